```python
import jax
import jax.numpy as jnp
from jax import lax
import numpy as np

D_MODEL = 1024
BATCH = 2
SEQ = 8192
DEPTH = 2
DEC_BATCH = 32
DEC_SEQ = 8
PAST_LEN = 8192
PAGE_SIZE = 128

N_META = 16
N_BRANCHES = 3
SB_HEADS = 8
SB_HEAD_DIM = 64
SB_WIDTH = SB_HEADS * SB_HEAD_DIM
ML_HEADS = 8
ML_HEAD_DIM = 64
ML_WIDTH = ML_HEADS * ML_HEAD_DIM
CONV_DIM = 512
CONV_WIDTH = 3
D_FF = 2816
Q_BLOCK = 128
ML_CHUNK = 128
RMS_EPS = 1e-6
IN_SIZES = (SB_WIDTH, SB_WIDTH, SB_WIDTH, ML_WIDTH, ML_WIDTH, ML_WIDTH, ML_HEADS, ML_HEADS, ML_WIDTH,
            CONV_DIM, CONV_DIM, CONV_DIM, N_BRANCHES * D_MODEL)
D_IN = sum(IN_SIZES)

kernel_name = "hybrid_stickbreak_mlstm_shortconv_macaron_step"


def rms_norm(x, gain):
    xf = x.astype(jnp.float32)
    y = xf * lax.rsqrt(jnp.mean(xf * xf, axis=-1, keepdims=True) + RMS_EPS)
    return (y * gain.astype(jnp.float32)).astype(x.dtype)


def swiglu(x, w_up, w_down):
    gate, up = jnp.split(x @ w_up, 2, axis=-1)
    return (jax.nn.silu(gate) * up) @ w_down


def _split_cols(h):
    idx, acc = [], 0
    for s in IN_SIZES[:-1]:
        acc += s
        idx.append(acc)
    return jnp.split(h, idx, axis=-1)


def _sb_block(q, k, v, q_pos, k_pos, bias):
    z = (jnp.einsum('bqhd,bkhd->bhqk', q, k).astype(jnp.float32) * (SB_HEAD_DIM ** -0.5)
         + bias.astype(jnp.float32)[None, :, None, None])
    valid = k_pos[None, :] < q_pos[:, None]
    log_beta = jax.nn.log_sigmoid(z)
    log_keep = jnp.where(valid, jax.nn.log_sigmoid(-z), 0.0)
    log_rest = lax.cumsum(log_keep, axis=3, reverse=True) - log_keep
    w = jnp.where(valid, jnp.exp(log_beta + log_rest), 0.0)
    return jnp.einsum('bhqk,bkhd->bqhd', w.astype(v.dtype), v)


def sb_attention(q, k, v, q_pos, k_pos, bias):
    bsz, tq, nh, hd = q.shape
    if tq <= Q_BLOCK:
        return _sb_block(q, k, v, q_pos, k_pos, bias)
    n_blk = -(-tq // Q_BLOCK)
    pad = n_blk * Q_BLOCK - tq
    qp = jnp.pad(q, ((0, 0), (0, pad), (0, 0), (0, 0)))
    posp = jnp.concatenate([q_pos, q_pos[-1] + 1 + jnp.arange(pad, dtype=jnp.int32)])
    qb = jnp.moveaxis(qp.reshape(bsz, n_blk, Q_BLOCK, nh, hd), 1, 0)
    pb = posp.reshape(n_blk, Q_BLOCK)
    out = lax.map(lambda a: _sb_block(a[0], k, v, a[1], k_pos, bias), (qb, pb))
    return jnp.moveaxis(out, 0, 1).reshape(bsz, n_blk * Q_BLOCK, nh, hd)[:, :tq]


def _mlstm_chunk(state, chunk):
    C, n, m = state
    q, k, v, log_i, log_f = chunk
    L = q.shape[1]
    b = jnp.cumsum(log_f, axis=1).transpose(0, 2, 1)
    li = log_i.transpose(0, 2, 1)
    causal = jnp.tril(jnp.ones((L, L), dtype=bool))
    log_d = jnp.where(causal, b[..., :, None] - b[..., None, :] + li[..., None, :], -jnp.inf)
    log_inter = b + m[..., None]
    m_t = jnp.maximum(log_inter, jnp.max(log_d, axis=-1))
    w_intra = jnp.exp(log_d - m_t[..., None])
    w_inter = jnp.exp(log_inter - m_t)
    s = jnp.einsum('blhd,bshd->bhls', q, k) * w_intra
    num = jnp.einsum('bhls,bshe->bhle', s, v) + w_inter[..., None] * jnp.einsum('blhd,bhde->bhle', q, C)
    den = jnp.sum(s, axis=-1) + w_inter * jnp.einsum('blhd,bhd->bhl', q, n)
    h = num / jnp.maximum(jnp.abs(den), jnp.exp(-m_t))[..., None]
    m_new = m_t[..., -1]
    w_end = jnp.exp(b[..., -1:] - b + li - m_new[..., None])
    decay = jnp.exp(b[..., -1] + m - m_new)
    C_new = decay[..., None, None] * C + jnp.einsum('bhs,bshd,bshe->bhde', w_end, k, v)
    n_new = decay[..., None] * n + jnp.einsum('bhs,bshd->bhd', w_end, k)
    return (C_new, n_new, m_new), h.transpose(0, 2, 1, 3)


def mlstm_run(state, q, k, v, log_i, log_f, lead):
    xs = (q, k, v, log_i, log_f)
    outs = []
    if lead > 0:
        state, h0 = _mlstm_chunk(state, tuple(a[:, :lead] for a in xs))
        outs.append(h0)
        xs = tuple(a[:, lead:] for a in xs)
    T = xs[0].shape[1]
    L = ML_CHUNK if T % ML_CHUNK == 0 else T
    n_ch = T // L
    chunks = tuple(jnp.moveaxis(a.reshape((a.shape[0], n_ch, L) + a.shape[2:]), 1, 0) for a in xs)
    state, h = lax.scan(_mlstm_chunk, state, chunks)
    h = jnp.moveaxis(h, 0, 1).reshape((h.shape[1], T) + h.shape[3:])
    outs.append(h)
    return state, jnp.concatenate(outs, axis=1)


def short_conv(u, buf, w):
    T = u.shape[1]
    up = jnp.concatenate([buf.astype(u.dtype), u], axis=1)
    y = w[0] * up[:, 0:T]
    for j in range(1, CONV_WIDTH):
        y = y + w[j] * up[:, j:j + T]
    return y, up[:, T:]


def token_mixer(xn, past_k, past_v, q_pos, k_pos, ml_state, conv_buf, lead,
                w_in, sb_bias, ml_i_bias, ml_f_bias, ml_head_g, conv_w, w_br_sb, w_br_ml, w_br_cv, w_out):
    bsz, T, _ = xn.shape
    f32 = jnp.float32
    (sq, sk, sv, mq, mk, mv, mi, mf, mo, cb, cc, ch, gates) = _split_cols(xn @ w_in)
    heads = lambda a, nh: a.reshape(bsz, T, nh, -1)
    k_new, v_new = heads(sk, SB_HEADS), heads(sv, SB_HEADS)
    if past_k is None:
        k_all, v_all = k_new, v_new
    else:
        k_all = jnp.concatenate([past_k.astype(k_new.dtype), k_new], axis=1)
        v_all = jnp.concatenate([past_v.astype(v_new.dtype), v_new], axis=1)
    y_sb = sb_attention(heads(sq, SB_HEADS), k_all, v_all, q_pos, k_pos, sb_bias).reshape(bsz, T, SB_WIDTH)
    log_i = mi.astype(f32) + ml_i_bias.astype(f32)
    log_f = jax.nn.log_sigmoid(mf.astype(f32) + ml_f_bias.astype(f32))
    ml_state, h = mlstm_run(ml_state, heads(mq, ML_HEADS).astype(f32),
                            heads(mk, ML_HEADS).astype(f32) * (ML_HEAD_DIM ** -0.5),
                            heads(mv, ML_HEADS).astype(f32), log_i, log_f, lead)
    h = h * lax.rsqrt(jnp.mean(h * h, axis=-1, keepdims=True) + RMS_EPS)
    y_ml = (jax.nn.sigmoid(mo.astype(f32)) * h.reshape(bsz, T, ML_WIDTH) * ml_head_g.astype(f32)).astype(xn.dtype)
    conv_y, conv_buf = short_conv(cc * ch, conv_buf, conv_w)
    y_cv = cb * conv_y
    g = jax.nn.sigmoid(gates.astype(f32)).astype(xn.dtype).reshape(bsz, T, N_BRANCHES, D_MODEL)
    merged = (g[..., 0, :] * (y_sb @ w_br_sb) + g[..., 1, :] * (y_ml @ w_br_ml)
              + g[..., 2, :] * (y_cv @ w_br_cv))
    return merged @ w_out, (k_new, v_new, ml_state[0], ml_state[1], ml_state[2], conv_buf)


def layer(x, lw, past_k, past_v, q_pos, k_pos, ml_state, conv_buf, lead):
    (g_ff1, ff1_up, ff1_down, g_mix, w_in, sb_bias, ml_i_bias, ml_f_bias, ml_head_g, conv_w,
     w_br_sb, w_br_ml, w_br_cv, w_out, g_ff2, ff2_up, ff2_down) = lw
    x = x + 0.5 * swiglu(rms_norm(x, g_ff1), ff1_up, ff1_down)
    y, new_state = token_mixer(rms_norm(x, g_mix), past_k, past_v, q_pos, k_pos, ml_state, conv_buf, lead,
                               w_in, sb_bias, ml_i_bias, ml_f_bias, ml_head_g, conv_w,
                               w_br_sb, w_br_ml, w_br_cv, w_out)
    x = x + y
    x = x + 0.5 * swiglu(rms_norm(x, g_ff2), ff2_up, ff2_down)
    return x, new_state


def setup_inputs(seed: int = 0) -> dict:
    key = jax.random.key(seed)
    ks = jax.random.split(key, 32)
    nrm = lambda k, shape, scale: scale * jax.random.normal(k, shape, jnp.float32)
    n_pages = PAST_LEN // PAGE_SIZE
    n_used = DEC_BATCH * n_pages
    n_phys = n_used + n_used // 4
    page_table = jax.random.permutation(ks[4], n_phys)[:n_used].reshape(DEC_BATCH, n_pages).astype(jnp.int32)
    kv_shape = (DEPTH, n_phys, PAGE_SIZE, SB_HEADS, SB_HEAD_DIM)
    return {
        "x_prompt": nrm(ks[0], (BATCH, SEQ, D_MODEL), 1.0),
        "x_sample": nrm(ks[1], (DEC_BATCH, DEC_SEQ, D_MODEL), 1.0),
        "cache_sb_k": nrm(ks[2], kv_shape, 1.0),
        "cache_sb_v": nrm(ks[3], kv_shape, 1.0),
        "page_table": page_table,
        "state_ml_C": nrm(ks[5], (DEPTH, DEC_BATCH, ML_HEADS, ML_HEAD_DIM, ML_HEAD_DIM), 0.5),
        "state_ml_n": nrm(ks[6], (DEPTH, DEC_BATCH, ML_HEADS, ML_HEAD_DIM), 1.0),
        "state_ml_m": nrm(ks[7], (DEPTH, DEC_BATCH, ML_HEADS), 0.5),
        "state_conv": nrm(ks[8], (DEPTH, DEC_BATCH, CONV_WIDTH - 1, CONV_DIM), 1.0),
        "meta_tokens": nrm(ks[9], (N_META, D_MODEL), 1.0),
        "norm_ff1": 1.0 + nrm(ks[10], (DEPTH, D_MODEL), 0.01),
        "ffn1_w_up": nrm(ks[11], (DEPTH, D_MODEL, 2 * D_FF), D_MODEL ** -0.5),
        "ffn1_w_down": nrm(ks[12], (DEPTH, D_FF, D_MODEL), D_FF ** -0.5),
        "norm_mix": 1.0 + nrm(ks[13], (DEPTH, D_MODEL), 0.01),
        "w_in": nrm(ks[14], (DEPTH, D_MODEL, D_IN), D_MODEL ** -0.5),
        "sb_logit_bias": jnp.linspace(-9.0, -6.0, SB_HEADS, dtype=jnp.float32)[None, :] + nrm(ks[27], (DEPTH, SB_HEADS), 0.1),
        "ml_igate_bias": nrm(ks[15], (DEPTH, ML_HEADS), 0.1),
        "ml_fgate_bias": jnp.linspace(3.0, 6.0, ML_HEADS, dtype=jnp.float32)[None, :] + nrm(ks[16], (DEPTH, ML_HEADS), 0.1),
        "ml_head_norm": 1.0 + nrm(ks[17], (DEPTH, ML_WIDTH), 0.01),
        "conv_w": nrm(ks[18], (DEPTH, CONV_WIDTH, CONV_DIM), CONV_WIDTH ** -0.5),
        "w_branch_sb": nrm(ks[19], (DEPTH, SB_WIDTH, D_MODEL), SB_WIDTH ** -0.5),
        "w_branch_ml": nrm(ks[20], (DEPTH, ML_WIDTH, D_MODEL), ML_WIDTH ** -0.5),
        "w_branch_cv": nrm(ks[21], (DEPTH, CONV_DIM, D_MODEL), CONV_DIM ** -0.5),
        "w_out": nrm(ks[22], (DEPTH, D_MODEL, D_MODEL), D_MODEL ** -0.5),
        "norm_ff2": 1.0 + nrm(ks[23], (DEPTH, D_MODEL), 0.01),
        "ffn2_w_up": nrm(ks[24], (DEPTH, D_MODEL, 2 * D_FF), D_MODEL ** -0.5),
        "ffn2_w_down": nrm(ks[25], (DEPTH, D_FF, D_MODEL), D_FF ** -0.5),
        "norm_final": 1.0 + nrm(ks[26], (D_MODEL,), 0.01),
    }


def reference(x_prompt, x_sample, cache_sb_k, cache_sb_v, page_table, state_ml_C, state_ml_n, state_ml_m,
              state_conv, meta_tokens, norm_ff1, ffn1_w_up, ffn1_w_down, norm_mix, w_in, sb_logit_bias,
              ml_igate_bias, ml_fgate_bias, ml_head_norm, conv_w, w_branch_sb, w_branch_ml, w_branch_cv, w_out,
              norm_ff2, ffn2_w_up, ffn2_w_down, norm_final):
    f32 = jnp.float32
    bp = x_prompt.shape[0]
    bs, ts = x_sample.shape[0], x_sample.shape[1]
    meta = jnp.broadcast_to(meta_tokens.astype(x_prompt.dtype)[None], (bp, N_META, D_MODEL))
    xp = jnp.concatenate([meta, x_prompt], axis=1)
    pos_p = jnp.arange(xp.shape[1], dtype=jnp.int32)
    past_len = page_table.shape[1] * PAGE_SIZE
    pos_sq = past_len + jnp.arange(ts, dtype=jnp.int32)
    pos_sk = jnp.arange(past_len + ts, dtype=jnp.int32)
    xs = x_sample
    new_p, new_s = [], []
    for l in range(DEPTH):
        lw = (norm_ff1[l], ffn1_w_up[l], ffn1_w_down[l], norm_mix[l], w_in[l], sb_logit_bias[l], ml_igate_bias[l],
              ml_fgate_bias[l], ml_head_norm[l], conv_w[l], w_branch_sb[l], w_branch_ml[l], w_branch_cv[l], w_out[l],
              norm_ff2[l], ffn2_w_up[l], ffn2_w_down[l])
        ml0 = (jnp.zeros((bp, ML_HEADS, ML_HEAD_DIM, ML_HEAD_DIM), f32),
               jnp.zeros((bp, ML_HEADS, ML_HEAD_DIM), f32), jnp.zeros((bp, ML_HEADS), f32))
        conv0 = jnp.zeros((bp, CONV_WIDTH - 1, CONV_DIM), xp.dtype)
        xp, st_p = layer(xp, lw, None, None, pos_p, pos_p, ml0, conv0, N_META)
        past_k = cache_sb_k[l][page_table].reshape(bs, past_len, SB_HEADS, SB_HEAD_DIM)
        past_v = cache_sb_v[l][page_table].reshape(bs, past_len, SB_HEADS, SB_HEAD_DIM)
        ml_s = (state_ml_C[l].astype(f32), state_ml_n[l].astype(f32), state_ml_m[l].astype(f32))
        xs, st_s = layer(xs, lw, past_k, past_v, pos_sq, pos_sk, ml_s, state_conv[l], 0)
        new_p.append(st_p)
        new_s.append(st_s)
    y_prompt = rms_norm(xp, norm_final)[:, N_META:]
    y_sample = rms_norm(xs, norm_final)
    stk = lambda states, i: jnp.stack([s[i] for s in states])
    return (y_prompt, y_sample,
            stk(new_p, 0), stk(new_p, 1), stk(new_p, 2), stk(new_p, 3), stk(new_p, 4), stk(new_p, 5),
            stk(new_s, 0), stk(new_s, 1), stk(new_s, 2), stk(new_s, 3), stk(new_s, 4), stk(new_s, 5))
```

```python
import functools

import jax
import jax.numpy as jnp
from jax import lax
from jax.experimental import pallas as pl
from jax.experimental.pallas import tpu as pltpu

F32 = jnp.float32
BF16 = jnp.bfloat16

D_MODEL = 1024
N_META = 16
HEADS = 8
HEAD_DIM = 64
WIDTH = HEADS * HEAD_DIM
PAIR = 2 * HEAD_DIM
N_PAIRS = HEADS // 2
PAGE_SIZE = 128
RMS_EPS = 1e-6
QK_SCALE = HEAD_DIM ** -0.5
NEG_BIG = -1e30

FF_CHUNK = 256
ROW_TILE_FFN = 512
ROW_TILE_PROJ = 256
ATT_BLOCK = 256
ML_CHUNK = 128
N_SEG = 16
GATE_LANES = 128
VMEM_LIMIT = 56 * 1024 * 1024

_NT = (((1,), (1,)), ((), ()))


def _params(semantics):
    return pltpu.CompilerParams(dimension_semantics=semantics, vmem_limit_bytes=VMEM_LIMIT)


def _resident(shape):
    nd = len(shape)
    return pl.BlockSpec(shape, lambda *_: (0,) * nd, pipeline_mode=pl.Buffered(1))


def _rms(x, gain):
    return x * lax.rsqrt(jnp.mean(x * x, axis=-1, keepdims=True) + RMS_EPS) * gain


def _log_sigmoid(x):
    return jnp.minimum(x, 0.0) - jnp.log1p(jnp.exp(-jnp.abs(x)))


def _row_tile(m, target):
    return target if m % target == 0 else m


def _ffn_body(*refs, n_chunks, final):
    if final:
        x_ref, g_ref, wup_ref, wdn_ref, gf_ref, o_ref, xn_ref, acc_ref = refs
    else:
        x_ref, g_ref, wup_ref, wdn_ref, o_ref, xn_ref, acc_ref = refs
    xn_ref[...] = _rms(x_ref[...], g_ref[...]).astype(BF16)
    acc_ref[...] = jnp.zeros_like(acc_ref)

    def chunk(c, carry):
        gu = jnp.dot(xn_ref[...], wup_ref[c], preferred_element_type=F32)
        gate, up = gu[:, :FF_CHUNK], gu[:, FF_CHUNK:]
        act = (gate * jax.nn.sigmoid(gate) * up).astype(BF16)
        acc_ref[...] += jnp.dot(act, wdn_ref[c], preferred_element_type=F32)
        return carry

    lax.fori_loop(0, n_chunks, chunk, 0)
    y = x_ref[...] + 0.5 * acc_ref[...]
    if final:
        y = _rms(y, gf_ref[...])
    o_ref[...] = y


def _ffn(x, gain, w_up, w_down, final_gain=None):
    m = x.shape[0]
    tm = _row_tile(m, ROW_TILE_FFN)
    n_chunks = w_up.shape[0]
    final = final_gain is not None
    row = lambda i: (i, 0)
    in_specs = [pl.BlockSpec((tm, D_MODEL), row), _resident((1, D_MODEL)),
                _resident(w_up.shape), _resident(w_down.shape)]
    args = [x, gain, w_up, w_down]
    if final:
        in_specs.append(_resident((1, D_MODEL)))
        args.append(final_gain)
    return pl.pallas_call(
        functools.partial(_ffn_body, n_chunks=n_chunks, final=final),
        grid=(m // tm,),
        in_specs=in_specs,
        out_specs=pl.BlockSpec((tm, D_MODEL), row),
        out_shape=jax.ShapeDtypeStruct((m, D_MODEL), F32),
        scratch_shapes=[pltpu.VMEM((tm, D_MODEL), BF16), pltpu.VMEM((tm, D_MODEL), F32)],
        compiler_params=_params(("parallel",)),
        name="ffn",
    )(*args)


def _inproj_body(x_ref, g_ref, wseg_ref, wgate_ref,
                 qa_ref, kn_ref, ka_ref, vn_ref, va_ref, mq_ref, mk_ref, mv_ref,
                 mo_ref, cb_ref, u_ref, mg_ref, ig_ref, xn_ref):
    xn_ref[...] = _rms(x_ref[...], g_ref[...]).astype(BF16)
    seg = lambda s: jnp.dot(xn_ref[...], wseg_ref[s], preferred_element_type=F32)
    qa_ref[...] = (seg(0) * QK_SCALE).astype(BF16)
    k = seg(1)
    kn_ref[...] = k
    ka_ref[...] = k.astype(BF16)
    v = seg(2)
    vn_ref[...] = v
    va_ref[...] = v.astype(BF16)
    mq_ref[...] = seg(3)
    mk_ref[...] = seg(4) * QK_SCALE
    mv_ref[...] = seg(5)
    mo_ref[...] = jax.nn.sigmoid(seg(6))
    cb_ref[...] = seg(7)
    u_ref[...] = seg(8) * seg(9)
    for s in range(6):
        mg_ref[:, s * WIDTH:(s + 1) * WIDTH] = jax.nn.sigmoid(seg(10 + s))
    ig_ref[...] = jnp.dot(xn_ref[...], wgate_ref[...], preferred_element_type=F32)


def _in_proj(x, gain, w_seg, w_gate):
    m = x.shape[0]
    tm = _row_tile(m, ROW_TILE_PROJ)
    row = lambda i: (i, 0)
    wide = lambda dt: jax.ShapeDtypeStruct((m, WIDTH), dt)
    out_shape = [wide(BF16), wide(F32), wide(BF16), wide(F32), wide(BF16),
                 wide(F32), wide(F32), wide(F32), wide(F32), wide(F32), wide(F32),
                 jax.ShapeDtypeStruct((m, 3 * D_MODEL), F32),
                 jax.ShapeDtypeStruct((m, GATE_LANES), F32)]
    out_specs = [pl.BlockSpec((tm, WIDTH), row)] * 11 + [
        pl.BlockSpec((tm, 3 * D_MODEL), row), pl.BlockSpec((tm, GATE_LANES), row)]
    return pl.pallas_call(
        _inproj_body,
        grid=(m // tm,),
        in_specs=[pl.BlockSpec((tm, D_MODEL), row), _resident((1, D_MODEL)),
                  _resident(w_seg.shape), _resident(w_gate.shape)],
        out_specs=out_specs,
        out_shape=out_shape,
        scratch_shapes=[pltpu.VMEM((tm, D_MODEL), BF16)],
        compiler_params=_params(("parallel",)),
        name="in_proj",
    )(x, gain, w_seg, w_gate)


def _strict_upper_stacked(blk):
    r = lax.broadcasted_iota(jnp.int32, (2 * blk, blk), 0)
    c = lax.broadcasted_iota(jnp.int32, (2 * blk, blk), 1)
    r = jnp.where(r >= blk, r - blk, r)
    return jnp.where(r > c, 1.0, 0.0).astype(BF16)


def _sb_step(qs, k, v, bias_t, uu, carry_ref, acc_ref, valid):
    z = lax.dot_general(qs, k, _NT, preferred_element_type=F32) + bias_t
    log_beta = _log_sigmoid(z)
    log_keep = log_beta - z
    if valid is not None:
        log_keep = jnp.where(valid, log_keep, 0.0)
    hi = log_keep.astype(BF16)
    lo = (log_keep - hi.astype(F32)).astype(BF16)
    rest = jnp.dot(jnp.concatenate([hi, lo], axis=1), uu, preferred_element_type=F32)
    w = jnp.exp(log_beta + rest + carry_ref[...])
    if valid is not None:
        w = jnp.where(valid, w, 0.0)
    carry_ref[...] += rest[:, :1] + log_keep[:, :1]
    acc_ref[...] += jnp.dot(w.astype(BF16), v, preferred_element_type=F32)


def _sb_prompt_body(bias_ref, q_ref, k_ref, v_ref, o_ref, qs_ref, uu_ref, acc_ref, carry_ref, *, blk):
    p = pl.program_id(1)
    i = pl.program_id(2)
    lane = lax.broadcasted_iota(jnp.int32, (blk, PAIR), 1)
    q2 = q_ref[0]
    zero = jnp.zeros_like(q2)
    qs_ref[:blk] = jnp.where(lane < HEAD_DIM, q2, zero)
    qs_ref[blk:] = jnp.where(lane >= HEAD_DIM, q2, zero)
    uu_ref[...] = _strict_upper_stacked(blk)
    acc_ref[...] = jnp.zeros_like(acc_ref)
    carry_ref[...] = jnp.zeros_like(carry_ref)

    row = lax.broadcasted_iota(jnp.int32, (2 * blk, blk), 0)
    col = lax.broadcasted_iota(jnp.int32, (2 * blk, blk), 1)
    second = row >= blk
    bias_t = jnp.where(second, bias_ref[2 * p + 1], bias_ref[2 * p])
    q_pos = i * blk + jnp.where(second, row - blk, row)

    def block(jb, carry):
        kb = i - jb
        start = pl.multiple_of(kb * blk, blk)
        edge = jnp.logical_or(jb == 0, kb == 0)

        @pl.when(edge)
        def _():
            k_pos = kb * blk + col
            valid = jnp.logical_and(k_pos < q_pos, k_pos >= blk - N_META)
            _sb_step(qs_ref[...], k_ref[0, pl.ds(start, blk), :], v_ref[0, pl.ds(start, blk), :],
                     bias_t, uu_ref[...], carry_ref, acc_ref, valid)

        @pl.when(jnp.logical_not(edge))
        def _():
            _sb_step(qs_ref[...], k_ref[0, pl.ds(start, blk), :], v_ref[0, pl.ds(start, blk), :],
                     bias_t, uu_ref[...], carry_ref, acc_ref, None)

        return carry

    lax.fori_loop(0, i + 1, block, 0)
    o_ref[0] = jnp.where(lane < HEAD_DIM, acc_ref[:blk], acc_ref[blk:]).astype(BF16)


def _sb_prompt(q, k, v, bias, blk):
    bsz, tp, _ = q.shape
    nq = tp // blk
    return pl.pallas_call(
        functools.partial(_sb_prompt_body, blk=blk),
        grid=(bsz, N_PAIRS, nq),
        in_specs=[pl.BlockSpec(memory_space=pltpu.SMEM),
                  pl.BlockSpec((1, blk, PAIR), lambda b, p, i: (b, i, p)),
                  pl.BlockSpec((1, tp, PAIR), lambda b, p, i: (b, 0, p)),
                  pl.BlockSpec((1, tp, PAIR), lambda b, p, i: (b, 0, p))],
        out_specs=pl.BlockSpec((1, blk, PAIR), lambda b, p, i: (b, i, p)),
        out_shape=jax.ShapeDtypeStruct((bsz, tp, WIDTH), BF16),
        scratch_shapes=[pltpu.VMEM((2 * blk, PAIR), BF16), pltpu.VMEM((2 * blk, blk), BF16),
                        pltpu.VMEM((2 * blk, PAIR), F32), pltpu.VMEM((2 * blk, 1), F32)],
        compiler_params=_params(("parallel", "parallel", "arbitrary")),
        name="sb_prompt",
    )(bias, q, k, v)


def _sb_paged_body(pt_ref, bias_ref, q_ref, kn_ref, vn_ref, kc_ref, vc_ref, o_ref,
                   qbd_ref, bias_t_ref, uu_ref, acc_ref, carry_ref, *, tq, n_steps):
    del pt_ref
    j = pl.program_id(1)
    rows = HEADS * tq
    row = lax.broadcasted_iota(jnp.int32, (rows, PAGE_SIZE), 0)
    col = lax.broadcasted_iota(jnp.int32, (rows, PAGE_SIZE), 1)

    @pl.when(j == 0)
    def _():
        rw = lax.broadcasted_iota(jnp.int32, (rows, WIDTH), 0)
        cw = lax.broadcasted_iota(jnp.int32, (rows, WIDTH), 1)
        q_all = jnp.concatenate([q_ref[0]] * HEADS, axis=0)
        qbd_ref[...] = jnp.where(rw // tq == cw // HEAD_DIM, q_all, 0.0).astype(BF16)
        bias_t = jnp.zeros((rows, PAGE_SIZE), F32)
        for h in range(HEADS):
            bias_t = jnp.where(row // tq == h, bias_ref[h], bias_t)
        bias_t_ref[...] = bias_t
        uu_ref[...] = _strict_upper_stacked(PAGE_SIZE)
        acc_ref[...] = jnp.zeros_like(acc_ref)
        carry_ref[...] = jnp.zeros_like(carry_ref)
        valid = col < row % tq
        _sb_step(qbd_ref[...], kn_ref[0].astype(BF16), vn_ref[0].astype(BF16),
                 bias_t, uu_ref[...], carry_ref, acc_ref, valid)

    @pl.when(j > 0)
    def _():
        _sb_step(qbd_ref[...], kc_ref[0, 0].astype(BF16), vc_ref[0, 0].astype(BF16),
                 bias_t_ref[...], uu_ref[...], carry_ref, acc_ref, None)

    @pl.when(j == n_steps - 1)
    def _():
        rw = lax.broadcasted_iota(jnp.int32, (tq, WIDTH), 1)
        out = jnp.zeros((tq, WIDTH), F32)
        for h in range(HEADS):
            out = jnp.where(rw // HEAD_DIM == h, acc_ref[h * tq:(h + 1) * tq], out)
        o_ref[0] = out


def _sb_paged(q, k_new, v_new, cache_k, cache_v, page_table, bias, layer):
    bsz, tq, _ = q.shape
    n_pages = page_table.shape[1]
    n_steps = n_pages + 1
    rows = HEADS * tq
    page = lambda b, j, pt: (layer, pt[b, n_pages - jnp.maximum(j, 1)], 0, 0)
    per_b = lambda b, j, pt: (b, 0, 0)
    grid_spec = pltpu.PrefetchScalarGridSpec(
        num_scalar_prefetch=1,
        grid=(bsz, n_steps),
        in_specs=[pl.BlockSpec(memory_space=pltpu.SMEM),
                  pl.BlockSpec((1, tq, WIDTH), per_b),
                  pl.BlockSpec((1, PAGE_SIZE, WIDTH), per_b),
                  pl.BlockSpec((1, PAGE_SIZE, WIDTH), per_b),
                  pl.BlockSpec((1, 1, PAGE_SIZE, WIDTH), page),
                  pl.BlockSpec((1, 1, PAGE_SIZE, WIDTH), page)],
        out_specs=pl.BlockSpec((1, tq, WIDTH), per_b),
        scratch_shapes=[pltpu.VMEM((rows, WIDTH), BF16), pltpu.VMEM((rows, PAGE_SIZE), F32),
                        pltpu.VMEM((2 * PAGE_SIZE, PAGE_SIZE), BF16),
                        pltpu.VMEM((rows, WIDTH), F32), pltpu.VMEM((rows, 1), F32)])
    return pl.pallas_call(
        functools.partial(_sb_paged_body, tq=tq, n_steps=n_steps),
        grid_spec=grid_spec,
        out_shape=jax.ShapeDtypeStruct((bsz, tq, WIDTH), F32),
        compiler_params=_params(("parallel", "arbitrary")),
        name="sb_paged",
    )(page_table, bias, q, k_new, v_new, cache_k, cache_v)


def _mlstm_body(q_ref, k_ref, v_ref, og_ref, gc_ref, gr_ref, bc_ref, br_ref, hg_ref,
                c0_ref, n0_ref, m0_ref, y_ref, c_out_ref, n_out_ref, m_out_ref,
                c_s, n_s, m_s, *, chunk, n_valid, n_chunks):
    ci = pl.program_id(1)

    @pl.when(ci == 0)
    def _():
        c_s[...] = c0_ref[0]
        n_s[...] = n0_ref[0]
        m_s[...] = m0_ref[0]

    L = chunk
    r_i = lax.broadcasted_iota(jnp.int32, (L, L), 0)
    c_i = lax.broadcasted_iota(jnp.int32, (L, L), 1)
    causal = r_i >= c_i
    tri = jnp.where(causal, 1.0, 0.0).astype(F32)
    tri_t = jnp.where(r_i <= c_i, 1.0, 0.0).astype(F32)

    gate_c = gc_ref[0] + bc_ref[...]
    gate_r = gr_ref[0][:2 * HEADS] + br_ref[...]
    lane_g = lax.broadcasted_iota(jnp.int32, (L, GATE_LANES), 1)
    row_g = lax.broadcasted_iota(jnp.int32, (2 * HEADS, L), 0)
    x_c = jnp.where(lane_g >= HEADS, _log_sigmoid(gate_c), gate_c)
    x_r = jnp.where(row_g >= HEADS, _log_sigmoid(gate_r), gate_r)
    if n_valid < L:
        tok_c = lax.broadcasted_iota(jnp.int32, (L, GATE_LANES), 0)
        tok_r = lax.broadcasted_iota(jnp.int32, (2 * HEADS, L), 1)
        x_c = jnp.where(tok_c < n_valid, x_c, jnp.where(lane_g >= HEADS, 0.0, NEG_BIG))
        x_r = jnp.where(tok_r < n_valid, x_r, jnp.where(row_g >= HEADS, 0.0, NEG_BIG))
    cum_c = jnp.dot(tri, x_c, preferred_element_type=F32, precision=lax.Precision.HIGHEST)
    cum_r = jnp.dot(x_r, tri_t, preferred_element_type=F32, precision=lax.Precision.HIGHEST)

    lane = lax.broadcasted_iota(jnp.int32, (L, PAIR), 1)
    lane1 = lax.broadcasted_iota(jnp.int32, (1, PAIR), 1)
    blk_r = lax.broadcasted_iota(jnp.int32, (PAIR, PAIR), 0)
    blk_c = lax.broadcasted_iota(jnp.int32, (PAIR, PAIR), 1)
    same_head = (blk_r < HEAD_DIM) == (blk_c < HEAD_DIM)
    lane_m = lax.broadcasted_iota(jnp.int32, (1, HEADS), 1)
    m_old = m_s[...]
    m_next = m_old

    for p in range(N_PAIRS):
        cols = slice(p * PAIR, (p + 1) * PAIR)
        q2 = q_ref[0, :, cols]
        k2 = k_ref[0, :, cols]
        v2b = v_ref[0, :, cols].astype(BF16)
        k2b = k2.astype(BF16)
        c_pair = c_s[p]
        c_pair_b = c_pair.astype(BF16)
        n_row = n_s[p]
        h_pair = None
        wend_pair = None
        decay_pair = None
        for hh in range(2):
            h = 2 * p + hh
            mine = (lane >= HEAD_DIM) if hh else (lane < HEAD_DIM)
            b_c = cum_c[:, HEADS + h:HEADS + h + 1]
            i_c = x_c[:, h:h + 1]
            b_r = cum_r[HEADS + h:HEADS + h + 1, :]
            i_r = x_r[h:h + 1, :]
            m_prev = m_old[:, h:h + 1]
            log_d = jnp.where(causal, b_c - b_r + i_r, NEG_BIG)
            inter = b_c + m_prev
            m_t = jnp.maximum(inter, jnp.max(log_d, axis=1, keepdims=True))
            w_intra = jnp.exp(log_d - m_t)
            w_inter = jnp.exp(inter - m_t)
            qm = jnp.where(mine, q2, 0.0)
            qmb = qm.astype(BF16)
            s = lax.dot_general(qmb, k2b, _NT, preferred_element_type=F32) * w_intra
            num = (jnp.dot(s.astype(BF16), v2b, preferred_element_type=F32)
                   + w_inter * jnp.dot(qmb, c_pair_b, preferred_element_type=F32))
            den = (jnp.sum(s, axis=1, keepdims=True)
                   + w_inter * jnp.sum(qm * n_row, axis=1, keepdims=True))
            hid = num * (1.0 / jnp.maximum(jnp.abs(den), jnp.exp(-m_t)))
            ms = jnp.sum(jnp.where(mine, hid * hid, 0.0), axis=1, keepdims=True) * (1.0 / HEAD_DIM)
            hid = hid * lax.rsqrt(ms + RMS_EPS)
            b_last = b_r[:, L - 1:L]
            g_r = b_last - b_r + i_r
            m_new = jnp.maximum(b_last + m_prev, jnp.max(g_r, axis=1, keepdims=True))
            wend = jnp.exp(b_last - b_c + i_c - m_new)
            decay = jnp.exp(b_last + m_prev - m_new)
            m_next = jnp.where(lane_m == h, m_new, m_next)
            if hh == 0:
                h_pair, wend_pair, decay_pair = hid, wend, decay
            else:
                h_pair = jnp.where(mine, hid, h_pair)
                wend_pair = jnp.where(mine, wend, wend_pair)
                decay_pair = jnp.where(lane1 >= HEAD_DIM, decay, decay_pair)
        y_ref[0, :, cols] = (og_ref[0, :, cols] * h_pair * hg_ref[:, cols]).astype(BF16)
        kw = k2 * wend_pair
        upd = jnp.dot(kw.T.astype(BF16), v2b, preferred_element_type=F32)
        c_s[p] = decay_pair * c_pair + jnp.where(same_head, upd, 0.0)
        n_s[p] = decay_pair * n_row + jnp.sum(kw, axis=0, keepdims=True)
    m_s[...] = m_next

    @pl.when(ci == n_chunks - 1)
    def _():
        c_out_ref[0] = c_s[...]
        n_out_ref[0] = n_s[...]
        m_out_ref[0] = m_s[...]


def _pack_pairs(c):
    bsz = c.shape[0]
    c = c.reshape(bsz, N_PAIRS, 2, HEAD_DIM, HEAD_DIM)
    z = jnp.zeros_like(c[:, :, 0])
    top = jnp.concatenate([c[:, :, 0], z], axis=-1)
    bot = jnp.concatenate([z, c[:, :, 1]], axis=-1)
    return jnp.concatenate([top, bot], axis=-2)


def _unpack_pairs(cp):
    bsz = cp.shape[0]
    a = cp[:, :, :HEAD_DIM, :HEAD_DIM]
    b = cp[:, :, HEAD_DIM:, HEAD_DIM:]
    return jnp.stack([a, b], axis=2).reshape(bsz, HEADS, HEAD_DIM, HEAD_DIM)


def _mlstm(q, k, v, og, gates, bias_c, bias_r, head_gain, c0, n0, m0, n_valid):
    bsz, t, _ = q.shape
    L = ML_CHUNK
    n_chunks = t // L
    gates_r = jnp.swapaxes(gates, 1, 2)
    tok = lambda b, c: (b, c, 0)
    per_b3 = lambda b, c: (b, 0, 0)
    per_b4 = lambda b, c: (b, 0, 0, 0)
    y, c_out, n_out, m_out = pl.pallas_call(
        functools.partial(_mlstm_body, chunk=L, n_valid=n_valid, n_chunks=n_chunks),
        grid=(bsz, n_chunks),
        in_specs=[pl.BlockSpec((1, L, WIDTH), tok)] * 4 + [
            pl.BlockSpec((1, L, GATE_LANES), tok),
            pl.BlockSpec((1, GATE_LANES, L), lambda b, c: (b, 0, c)),
            pl.BlockSpec((1, GATE_LANES), lambda b, c: (0, 0)),
            pl.BlockSpec((2 * HEADS, 1), lambda b, c: (0, 0)),
            pl.BlockSpec((1, WIDTH), lambda b, c: (0, 0)),
            pl.BlockSpec((1, N_PAIRS, PAIR, PAIR), per_b4),
            pl.BlockSpec((1, N_PAIRS, 1, PAIR), per_b4),
            pl.BlockSpec((1, 1, HEADS), per_b3)],
        out_specs=[pl.BlockSpec((1, L, WIDTH), tok),
                   pl.BlockSpec((1, N_PAIRS, PAIR, PAIR), per_b4),
                   pl.BlockSpec((1, N_PAIRS, 1, PAIR), per_b4),
                   pl.BlockSpec((1, 1, HEADS), per_b3)],
        out_shape=[jax.ShapeDtypeStruct((bsz, t, WIDTH), BF16),
                   jax.ShapeDtypeStruct((bsz, N_PAIRS, PAIR, PAIR), F32),
                   jax.ShapeDtypeStruct((bsz, N_PAIRS, 1, PAIR), F32),
                   jax.ShapeDtypeStruct((bsz, 1, HEADS), F32)],
        scratch_shapes=[pltpu.VMEM((N_PAIRS, PAIR, PAIR), F32), pltpu.VMEM((N_PAIRS, 1, PAIR), F32),
                        pltpu.VMEM((1, HEADS), F32)],
        compiler_params=_params(("parallel", "arbitrary")),
        name="mlstm",
    )(q, k, v, og, gates, gates_r, bias_c, bias_r, head_gain,
      _pack_pairs(c0), n0.reshape(bsz, N_PAIRS, 1, PAIR), m0.reshape(bsz, 1, HEADS))
    return (y, _unpack_pairs(c_out), n_out.reshape(bsz, HEADS, HEAD_DIM), m_out.reshape(bsz, HEADS))


def _merge_body(x_ref, ysb_ref, yml_ref, cb_ref, u_ref, u1_ref, u2_ref, mg_ref,
                cw_ref, wbr_ref, wout_ref, o_ref):
    cw = cw_ref[...]
    conv = cw[0:1] * u2_ref[...] + cw[1:2] * u1_ref[...] + cw[2:3] * u_ref[...]
    ycv = (cb_ref[...] * conv).astype(BF16)
    merged = (mg_ref[:, :D_MODEL] * jnp.dot(ysb_ref[...], wbr_ref[0], preferred_element_type=F32)
              + mg_ref[:, D_MODEL:2 * D_MODEL] * jnp.dot(yml_ref[...], wbr_ref[1], preferred_element_type=F32)
              + mg_ref[:, 2 * D_MODEL:] * jnp.dot(ycv, wbr_ref[2], preferred_element_type=F32))
    o_ref[...] = x_ref[...] + jnp.dot(merged.astype(BF16), wout_ref[...], preferred_element_type=F32)


def _merge(x, ysb, yml, cb, u, u1, u2, mg, conv_w, w_br, w_out):
    m = x.shape[0]
    tm = _row_tile(m, ROW_TILE_PROJ)
    row = lambda i: (i, 0)
    wide = pl.BlockSpec((tm, WIDTH), row)
    return pl.pallas_call(
        _merge_body,
        grid=(m // tm,),
        in_specs=[pl.BlockSpec((tm, D_MODEL), row), wide, wide, wide, wide, wide, wide,
                  pl.BlockSpec((tm, 3 * D_MODEL), row),
                  _resident(conv_w.shape), _resident(w_br.shape), _resident(w_out.shape)],
        out_specs=pl.BlockSpec((tm, D_MODEL), row),
        out_shape=jax.ShapeDtypeStruct((m, D_MODEL), F32),
        compiler_params=_params(("parallel",)),
        name="merge",
    )(x, ysb, yml, cb, u, u1, u2, mg, conv_w, w_br, w_out)


def _shifted(u, buf):
    ext = jnp.concatenate([buf, u], axis=1)
    t = u.shape[1]
    return ext[:, 1:t + 1], ext[:, 0:t]


def _pad_tokens(a, t_to):
    return jnp.pad(a, ((0, 0), (0, t_to - a.shape[1]), (0, 0)))


def kernel(x_prompt, x_sample, cache_sb_k, cache_sb_v, page_table, state_ml_C, state_ml_n, state_ml_m,
           state_conv, meta_tokens, norm_ff1, ffn1_w_up, ffn1_w_down, norm_mix, w_in, sb_logit_bias,
           ml_igate_bias, ml_fgate_bias, ml_head_norm, conv_w, w_branch_sb, w_branch_ml, w_branch_cv, w_out,
           norm_ff2, ffn2_w_up, ffn2_w_down, norm_final):
    bp, seq, _ = x_prompt.shape
    bs, ts, _ = x_sample.shape
    depth = w_in.shape[0]
    d_ff = ffn1_w_down.shape[1]
    n_ff = d_ff // FF_CHUNK
    n_small = bs * ts
    blk = ATT_BLOCK
    assert seq % blk == 0 and seq % ML_CHUNK == 0 and d_ff % FF_CHUNK == 0
    assert ts % 8 == 0 and ts <= ML_CHUNK and N_META <= ML_CHUNK

    def ffn_weights(w_up, w_down):
        gate = w_up[:, :d_ff].reshape(D_MODEL, n_ff, FF_CHUNK)
        up = w_up[:, d_ff:].reshape(D_MODEL, n_ff, FF_CHUNK)
        wu = jnp.concatenate([gate, up], axis=-1).transpose(1, 0, 2).astype(BF16)
        return wu, w_down.reshape(n_ff, FF_CHUNK, D_MODEL).astype(BF16)

    def in_weights(w):
        main = jnp.concatenate([w[:, :6 * WIDTH], w[:, 6 * WIDTH + 2 * HEADS:]], axis=1)
        w_seg = main.reshape(D_MODEL, N_SEG, WIDTH).transpose(1, 0, 2).astype(BF16)
        gate = w[:, 6 * WIDTH:6 * WIDTH + 2 * HEADS]
        w_gate = jnp.pad(gate, ((0, 0), (0, GATE_LANES - 2 * HEADS))).astype(BF16)
        return w_seg, w_gate

    row1 = lambda a: a.reshape(1, -1)
    xb = x_prompt.reshape(bp * seq, D_MODEL)
    xs = jnp.concatenate([x_sample.reshape(n_small, D_MODEL), meta_tokens.astype(F32)], axis=0)
    cache_k = cache_sb_k.reshape(cache_sb_k.shape[:3] + (WIDTH,))
    cache_v = cache_sb_v.reshape(cache_sb_v.shape[:3] + (WIDTH,))

    new_p, new_s = [], []
    for l in range(depth):
        last = l == depth - 1
        wu1, wd1 = ffn_weights(ffn1_w_up[l], ffn1_w_down[l])
        wu2, wd2 = ffn_weights(ffn2_w_up[l], ffn2_w_down[l])
        w_seg, w_gate = in_weights(w_in[l])
        w_br = jnp.stack([w_branch_sb[l], w_branch_ml[l], w_branch_cv[l]]).astype(BF16)
        w_o = w_out[l].astype(BF16)
        bias_c = jnp.pad(jnp.concatenate([ml_igate_bias[l], ml_fgate_bias[l]]),
                         (0, GATE_LANES - 2 * HEADS)).reshape(1, GATE_LANES)
        bias_r = jnp.concatenate([ml_igate_bias[l], ml_fgate_bias[l]]).reshape(2 * HEADS, 1)
        head_gain = row1(ml_head_norm[l])

        xb = _ffn(xb, row1(norm_ff1[l]), wu1, wd1)
        xs = _ffn(xs, row1(norm_ff1[l]), wu1, wd1)
        pb = _in_proj(xb, row1(norm_mix[l]), w_seg, w_gate)
        ps = _in_proj(xs, row1(norm_mix[l]), w_seg, w_gate)
        (qa_b, kn_b, ka_b, vn_b, va_b, mq_b, mk_b, mv_b, og_b, cb_b, u_b, mg_b, ig_b) = pb
        (qa_s, kn_s, ka_s, vn_s, va_s, mq_s, mk_s, mv_s, og_s, cb_s, u_s, mg_s, ig_s) = ps
        seq3 = lambda a: a.reshape(bp, seq, a.shape[-1])
        smp3 = lambda a: a[:n_small].reshape(bs, ts, a.shape[-1])
        meta3 = lambda a: a[n_small:].reshape(1, N_META, a.shape[-1])
        meta_b = lambda a: jnp.broadcast_to(meta3(a), (bp, N_META, a.shape[-1]))

        def padded(a_small, a_big):
            lead = jnp.zeros((bp, blk - N_META, WIDTH), a_big.dtype)
            return jnp.concatenate([lead, meta_b(a_small), seq3(a_big)], axis=1)

        att_p = _sb_prompt(padded(qa_s, qa_b), padded(ka_s, ka_b), padded(va_s, va_b), sb_logit_bias[l], blk)
        ysb_b = att_p[:, blk:].reshape(bp * seq, WIDTH)
        ysb_meta = att_p[0, blk - N_META:blk]
        att_s = _sb_paged(smp3(qa_s).astype(F32), _pad_tokens(smp3(kn_s), PAGE_SIZE),
                          _pad_tokens(smp3(vn_s), PAGE_SIZE), cache_k, cache_v, page_table,
                          sb_logit_bias[l], l)
        ysb_s = jnp.concatenate([att_s.reshape(n_small, WIDTH).astype(BF16), ysb_meta], axis=0)

        ml_args = (bias_c, bias_r, head_gain)
        zc = jnp.zeros((1, HEADS, HEAD_DIM, HEAD_DIM), F32)
        zn = jnp.zeros((1, HEADS, HEAD_DIM), F32)
        zm = jnp.zeros((1, HEADS), F32)
        padm = lambda a: _pad_tokens(meta3(a), ML_CHUNK)
        yml_meta, c_m, n_m, m_m = _mlstm(padm(mq_s), padm(mk_s), padm(mv_s), padm(og_s), padm(ig_s),
                                         *ml_args, zc, zn, zm, N_META)
        rep = lambda a: jnp.broadcast_to(a, (bp,) + a.shape[1:])
        yml_b, c_p, n_p, m_p = _mlstm(seq3(mq_b), seq3(mk_b), seq3(mv_b), seq3(og_b), seq3(ig_b),
                                      *ml_args, rep(c_m), rep(n_m), rep(m_m), ML_CHUNK)
        pads = lambda a: _pad_tokens(smp3(a), ML_CHUNK)
        yml_smp, c_s, n_s, m_s = _mlstm(pads(mq_s), pads(mk_s), pads(mv_s), pads(og_s), pads(ig_s),
                                        *ml_args, state_ml_C[l].astype(F32), state_ml_n[l].astype(F32),
                                        state_ml_m[l].astype(F32), ts)
        yml_s = jnp.concatenate([yml_smp[:, :ts].reshape(n_small, WIDTH), yml_meta[0, :N_META]], axis=0)

        u_meta = meta3(u_s)
        u1_meta, u2_meta = _shifted(u_meta, jnp.zeros((1, 2, WIDTH), F32))
        u_seq = seq3(u_b)
        u1_b, u2_b = _shifted(u_seq, rep(u_meta[:, -2:]))
        u_smp = smp3(u_s)
        u1_smp, u2_smp = _shifted(u_smp, state_conv[l].astype(F32))
        flat = lambda a: a.reshape(-1, WIDTH)
        u1_s = jnp.concatenate([flat(u1_smp), flat(u1_meta)], axis=0)
        u2_s = jnp.concatenate([flat(u2_smp), flat(u2_meta)], axis=0)

        xb = _merge(xb, ysb_b, yml_b.reshape(bp * seq, WIDTH), cb_b, u_b, flat(u1_b), flat(u2_b), mg_b,
                    conv_w[l], w_br, w_o)
        xs = _merge(xs, ysb_s, yml_s, cb_s, u_s, u1_s, u2_s, mg_s, conv_w[l], w_br, w_o)

        fin = row1(norm_final) if last else None
        xb = _ffn(xb, row1(norm_ff2[l]), wu2, wd2, fin)
        xs = _ffn(xs, row1(norm_ff2[l]), wu2, wd2, fin)

        heads = lambda a: a.reshape(a.shape[:-1] + (HEADS, HEAD_DIM))
        new_p.append((heads(jnp.concatenate([meta_b(kn_s), seq3(kn_b)], axis=1)),
                      heads(jnp.concatenate([meta_b(vn_s), seq3(vn_b)], axis=1)),
                      c_p, n_p, m_p, u_seq[:, -2:]))
        new_s.append((heads(smp3(kn_s)), heads(smp3(vn_s)), c_s, n_s, m_s,
                      jnp.concatenate([state_conv[l].astype(F32), u_smp], axis=1)[:, -2:]))

    y_prompt = xb.reshape(bp, seq, D_MODEL)
    y_sample = xs[:n_small].reshape(bs, ts, D_MODEL)
    stk = lambda states, i: jnp.stack([s[i] for s in states])
    return (y_prompt, y_sample) + tuple(stk(new_p, i) for i in range(6)) + tuple(stk(new_s, i) for i in range(6))
```

```python
import functools

import jax
import jax.numpy as jnp
from jax import lax
from jax.experimental import pallas as pl
from jax.experimental.pallas import tpu as pltpu

F32 = jnp.float32
BF16 = jnp.bfloat16

D_MODEL = 1024
N_META = 16
HEADS = 8
HEAD_DIM = 64
WIDTH = HEADS * HEAD_DIM
PAIR = 2 * HEAD_DIM
N_PAIRS = HEADS // 2
PAGE_SIZE = 128
RMS_EPS = 1e-6
QK_SCALE = HEAD_DIM ** -0.5
NEG_BIG = -1e30

FF_CHUNK = 256
ROW_TILE_FFN = 512
ROW_TILE_PROJ = 256
ATT_BLOCK = 256
ATT_PAIRS_PER_STEP = 4
ML_CHUNK = 128
PAGE_GROUP = 8
N_SEG = 16
GATE_LANES = 128
VMEM_LIMIT = 56 * 1024 * 1024

_NT = (((1,), (1,)), ((), ()))


def _params(semantics):
    return pltpu.CompilerParams(dimension_semantics=semantics, vmem_limit_bytes=VMEM_LIMIT)


def _resident(shape):
    nd = len(shape)
    return pl.BlockSpec(shape, lambda *_: (0,) * nd, pipeline_mode=pl.Buffered(1))


def _rms(x, gain):
    return x * lax.rsqrt(jnp.mean(x * x, axis=-1, keepdims=True) + RMS_EPS) * gain


def _log_sigmoid(x):
    return jnp.minimum(x, 0.0) - jnp.log1p(jnp.exp(-jnp.abs(x)))


def _row_tile(m, target):
    return target if m % target == 0 else m


def _ffn_body(*refs, n_chunks, final):
    if final:
        x_ref, g_ref, wup_ref, wdn_ref, gf_ref, o_ref, xn_ref, acc_ref = refs
    else:
        x_ref, g_ref, wup_ref, wdn_ref, o_ref, xn_ref, acc_ref = refs
    xn_ref[...] = _rms(x_ref[...], g_ref[...]).astype(BF16)
    acc_ref[...] = jnp.zeros_like(acc_ref)

    def chunk(c, carry):
        gu = jnp.dot(xn_ref[...], wup_ref[c], preferred_element_type=F32)
        gate, up = gu[:, :FF_CHUNK], gu[:, FF_CHUNK:]
        act = (gate * jax.nn.sigmoid(gate) * up).astype(BF16)
        acc_ref[...] += jnp.dot(act, wdn_ref[c], preferred_element_type=F32)
        return carry

    lax.fori_loop(0, n_chunks, chunk, 0)
    y = x_ref[...] + 0.5 * acc_ref[...]
    if final:
        y = _rms(y, gf_ref[...])
    o_ref[...] = y


def _ffn(x, gain, w_up, w_down, final_gain=None):
    m = x.shape[0]
    tm = _row_tile(m, ROW_TILE_FFN)
    n_chunks = w_up.shape[0]
    final = final_gain is not None
    row = lambda i: (i, 0)
    in_specs = [pl.BlockSpec((tm, D_MODEL), row), _resident((1, D_MODEL)),
                _resident(w_up.shape), _resident(w_down.shape)]
    args = [x, gain, w_up, w_down]
    if final:
        in_specs.append(_resident((1, D_MODEL)))
        args.append(final_gain)
    return pl.pallas_call(
        functools.partial(_ffn_body, n_chunks=n_chunks, final=final),
        grid=(m // tm,),
        in_specs=in_specs,
        out_specs=pl.BlockSpec((tm, D_MODEL), row),
        out_shape=jax.ShapeDtypeStruct((m, D_MODEL), F32),
        scratch_shapes=[pltpu.VMEM((tm, D_MODEL), BF16), pltpu.VMEM((tm, D_MODEL), F32)],
        compiler_params=_params(("parallel",)),
        name="ffn",
    )(*args)


def _inproj_body(x_ref, g_ref, wseg_ref, wgate_ref,
                 qa_ref, kn_ref, ka_ref, vn_ref, va_ref, mq_ref, mk_ref, mv_ref,
                 mo_ref, cb_ref, u_ref, mg_ref, ig_ref, xn_ref):
    xn_ref[...] = _rms(x_ref[...], g_ref[...]).astype(BF16)
    seg = lambda s: jnp.dot(xn_ref[...], wseg_ref[s], preferred_element_type=F32)
    qa_ref[...] = (seg(0) * QK_SCALE).astype(BF16)
    k = seg(1)
    kn_ref[...] = k
    ka_ref[...] = k.astype(BF16)
    v = seg(2)
    vn_ref[...] = v
    va_ref[...] = v.astype(BF16)
    mq_ref[...] = seg(3)
    mk_ref[...] = seg(4) * QK_SCALE
    mv_ref[...] = seg(5)
    mo_ref[...] = jax.nn.sigmoid(seg(6))
    cb_ref[...] = seg(7)
    u_ref[...] = seg(8) * seg(9)
    for s in range(6):
        mg_ref[:, s * WIDTH:(s + 1) * WIDTH] = jax.nn.sigmoid(seg(10 + s))
    ig_ref[...] = jnp.dot(xn_ref[...], wgate_ref[...], preferred_element_type=F32)


def _in_proj(x, gain, w_seg, w_gate):
    m = x.shape[0]
    tm = _row_tile(m, ROW_TILE_PROJ)
    row = lambda i: (i, 0)
    wide = lambda dt: jax.ShapeDtypeStruct((m, WIDTH), dt)
    out_shape = [wide(BF16), wide(F32), wide(BF16), wide(F32), wide(BF16),
                 wide(F32), wide(F32), wide(F32), wide(F32), wide(F32), wide(F32),
                 jax.ShapeDtypeStruct((m, 3 * D_MODEL), F32),
                 jax.ShapeDtypeStruct((m, GATE_LANES), F32)]
    out_specs = [pl.BlockSpec((tm, WIDTH), row)] * 11 + [
        pl.BlockSpec((tm, 3 * D_MODEL), row), pl.BlockSpec((tm, GATE_LANES), row)]
    return pl.pallas_call(
        _inproj_body,
        grid=(m // tm,),
        in_specs=[pl.BlockSpec((tm, D_MODEL), row), _resident((1, D_MODEL)),
                  _resident(w_seg.shape), _resident(w_gate.shape)],
        out_specs=out_specs,
        out_shape=out_shape,
        scratch_shapes=[pltpu.VMEM((tm, D_MODEL), BF16)],
        compiler_params=_params(("parallel",)),
        name="in_proj",
    )(x, gain, w_seg, w_gate)


def _strict_upper_stacked(blk):
    r = lax.broadcasted_iota(jnp.int32, (2 * blk, blk), 0)
    c = lax.broadcasted_iota(jnp.int32, (2 * blk, blk), 1)
    r = jnp.where(r >= blk, r - blk, r)
    return jnp.where(r > c, 1.0, 0.0).astype(BF16)


def _sb_logs(z, valid):
    soft = jnp.log(1.0 + jnp.exp(-jnp.abs(z)))
    log_beta = jnp.minimum(z, 0.0) - soft
    log_keep = log_beta - z
    if valid is not None:
        log_keep = jnp.where(valid, log_keep, 0.0)
    hi = log_keep.astype(BF16)
    lo = (log_keep - hi.astype(F32)).astype(BF16)
    return log_beta, log_keep, jnp.concatenate([hi, lo], axis=1)


def _sb_block_total(rest, log_keep):
    return rest[:, :1] + log_keep[:, :1]


def _sb_finish(log_beta, rest, carry, valid):
    w = jnp.exp(log_beta + rest + carry)
    if valid is not None:
        w = jnp.where(valid, w, 0.0)
    return w.astype(BF16)


def _sb_prompt_body(bias_ref, q_ref, k_ref, v_ref, o_ref, qs_ref, uu_ref, acc_ref, carry_ref, *, blk, n_par):
    first_pair = pl.program_id(1) * n_par
    i = pl.program_id(2)
    lane = lax.broadcasted_iota(jnp.int32, (blk, PAIR), 1)
    for a in range(n_par):
        q2 = q_ref[0, :, a * PAIR:(a + 1) * PAIR]
        zero = jnp.zeros_like(q2)
        qs_ref[a, :blk] = jnp.where(lane < HEAD_DIM, q2, zero)
        qs_ref[a, blk:] = jnp.where(lane >= HEAD_DIM, q2, zero)
    uu_ref[...] = _strict_upper_stacked(blk)
    acc_ref[...] = jnp.zeros_like(acc_ref)
    carry_ref[...] = jnp.zeros_like(carry_ref)
    bias = [(bias_ref[2 * (first_pair + a)], bias_ref[2 * (first_pair + a) + 1]) for a in range(n_par)]

    def step(kb, masked):
        start = pl.multiple_of(kb * blk, blk)
        pairs = range(n_par)
        cols = lambda a: slice(a * PAIR, (a + 1) * PAIR)
        s = [lax.dot_general(qs_ref[a], k_ref[0, pl.ds(start, blk), cols(a)], _NT, preferred_element_type=F32)
             for a in pairs]
        z = [jnp.concatenate([s[a][:blk] + bias[a][0], s[a][blk:] + bias[a][1]], axis=0) for a in pairs]
        valid = None
        if masked:
            row = lax.broadcasted_iota(jnp.int32, (2 * blk, blk), 0)
            col = lax.broadcasted_iota(jnp.int32, (2 * blk, blk), 1)
            q_pos = i * blk + jnp.where(row >= blk, row - blk, row)
            k_pos = kb * blk + col
            valid = jnp.logical_and(k_pos < q_pos, k_pos >= blk - N_META)
        logs = [_sb_logs(z[a], valid) for a in pairs]
        rest = [jnp.dot(logs[a][2], uu_ref[...], preferred_element_type=F32) for a in pairs]
        w = [_sb_finish(logs[a][0], rest[a], carry_ref[a], valid) for a in pairs]
        for a in pairs:
            carry_ref[a] += _sb_block_total(rest[a], logs[a][1])
            acc_ref[a] += jnp.dot(w[a], v_ref[0, pl.ds(start, blk), cols(a)], preferred_element_type=F32)

    def block(jb, carry):
        kb = i - jb
        edge = jnp.logical_or(jb == 0, kb == 0)
        pl.when(edge)(lambda: step(kb, True))
        pl.when(jnp.logical_not(edge))(lambda: step(kb, False))
        return carry

    lax.fori_loop(0, i + 1, block, 0)
    for a in range(n_par):
        o_ref[0, :, a * PAIR:(a + 1) * PAIR] = jnp.where(
            lane < HEAD_DIM, acc_ref[a, :blk], acc_ref[a, blk:]).astype(BF16)


def _sb_prompt(q, k, v, bias, blk):
    bsz, tp, _ = q.shape
    nq = tp // blk
    n_par = ATT_PAIRS_PER_STEP
    wide = n_par * PAIR
    return pl.pallas_call(
        functools.partial(_sb_prompt_body, blk=blk, n_par=n_par),
        grid=(bsz, N_PAIRS // n_par, nq),
        in_specs=[pl.BlockSpec(memory_space=pltpu.SMEM),
                  pl.BlockSpec((1, blk, wide), lambda b, p, i: (b, i, p)),
                  pl.BlockSpec((1, tp, wide), lambda b, p, i: (b, 0, p)),
                  pl.BlockSpec((1, tp, wide), lambda b, p, i: (b, 0, p))],
        out_specs=pl.BlockSpec((1, blk, wide), lambda b, p, i: (b, i, p)),
        out_shape=jax.ShapeDtypeStruct((bsz, tp, WIDTH), BF16),
        scratch_shapes=[pltpu.VMEM((n_par, 2 * blk, PAIR), BF16), pltpu.VMEM((2 * blk, blk), BF16),
                        pltpu.VMEM((n_par, 2 * blk, PAIR), F32), pltpu.VMEM((n_par, 2 * blk, 1), F32)],
        compiler_params=_params(("parallel", "parallel", "arbitrary")),
        name="sb_prompt",
    )(bias, q, k, v)


def _sb_paged_body(pt_ref, bias_ref, q_ref, kn_ref, vn_ref, *rest, tq, group, n_steps):
    del pt_ref
    k_refs = rest[:group]
    v_refs = rest[group:2 * group]
    o_ref, qbd_ref, bias_t_ref, uu_ref, acc_ref, carry_ref = rest[2 * group:]
    j = pl.program_id(1)
    rows = HEADS * tq

    def sweep(kts, vts, carry, acc, valid):
        n = len(kts)
        kt = jnp.concatenate([t.astype(BF16) for t in kts], axis=1)
        vt = jnp.concatenate([t.astype(BF16) for t in vts], axis=1)
        s = jnp.dot(qbd_ref[...], kt, preferred_element_type=F32)
        page = lambda a, g: a[:, g * PAGE_SIZE:(g + 1) * PAGE_SIZE]
        logs = [_sb_logs(page(s, g) + bias_t_ref[...], valid) for g in range(n)]
        rest = jnp.dot(jnp.concatenate([lg[2] for lg in logs], axis=0), uu_ref[...], preferred_element_type=F32)
        w = []
        for g in range(n):
            rest_g = rest[g * rows:(g + 1) * rows]
            w.append(_sb_finish(logs[g][0], rest_g, carry, valid))
            carry = carry + _sb_block_total(rest_g, logs[g][1])
        w = jnp.concatenate(w, axis=1)
        return carry, acc + lax.dot_general(w, vt, _NT, preferred_element_type=F32)

    @pl.when(j == 0)
    def _():
        rw = lax.broadcasted_iota(jnp.int32, (rows, WIDTH), 0)
        cw = lax.broadcasted_iota(jnp.int32, (rows, WIDTH), 1)
        q_all = jnp.concatenate([q_ref[0]] * HEADS, axis=0)
        qbd_ref[...] = jnp.where(rw // tq == cw // HEAD_DIM, q_all, 0.0).astype(BF16)
        row = lax.broadcasted_iota(jnp.int32, (rows, PAGE_SIZE), 0)
        col = lax.broadcasted_iota(jnp.int32, (rows, PAGE_SIZE), 1)
        bias_t = jnp.zeros((rows, PAGE_SIZE), F32)
        for h in range(HEADS):
            bias_t = jnp.where(row // tq == h, bias_ref[h], bias_t)
        bias_t_ref[...] = bias_t
        uu_ref[...] = _strict_upper_stacked(PAGE_SIZE)
        valid = col < row % tq
        carry, acc = sweep([kn_ref[0]], [vn_ref[0]], jnp.zeros((rows, 1), F32), jnp.zeros((rows, WIDTH), F32), valid)
        carry_ref[...] = carry
        acc_ref[...] = acc

    carry, acc = sweep([r[0, 0] for r in k_refs], [r[0, 0] for r in v_refs], carry_ref[...], acc_ref[...], None)
    carry_ref[...] = carry
    acc_ref[...] = acc

    @pl.when(j == n_steps - 1)
    def _():
        rw = lax.broadcasted_iota(jnp.int32, (tq, WIDTH), 1)
        out = jnp.zeros((tq, WIDTH), F32)
        for h in range(HEADS):
            out = jnp.where(rw // HEAD_DIM == h, acc_ref[h * tq:(h + 1) * tq], out)
        o_ref[0] = out


def _sb_paged(q, kt_new, vt_new, cache_kt, cache_vt, page_table, bias, layer):
    bsz, tq, _ = q.shape
    n_pages = page_table.shape[1]
    group = next(g for g in (PAGE_GROUP, 4, 2, 1) if n_pages % g == 0)
    n_steps = n_pages // group
    rows = HEADS * tq
    per_b = lambda b, j, pt: (b, 0, 0)

    def page(g):
        return lambda b, j, pt: (layer, pt[b, n_pages - 1 - (j * group + g)], 0, 0)

    page_specs = [pl.BlockSpec((1, 1, WIDTH, PAGE_SIZE), page(g)) for g in range(group)]
    grid_spec = pltpu.PrefetchScalarGridSpec(
        num_scalar_prefetch=1,
        grid=(bsz, n_steps),
        in_specs=[pl.BlockSpec(memory_space=pltpu.SMEM),
                  pl.BlockSpec((1, tq, WIDTH), per_b),
                  pl.BlockSpec((1, WIDTH, PAGE_SIZE), per_b),
                  pl.BlockSpec((1, WIDTH, PAGE_SIZE), per_b)] + page_specs + page_specs,
        out_specs=pl.BlockSpec((1, tq, WIDTH), per_b),
        scratch_shapes=[pltpu.VMEM((rows, WIDTH), BF16), pltpu.VMEM((rows, PAGE_SIZE), F32),
                        pltpu.VMEM((2 * PAGE_SIZE, PAGE_SIZE), BF16),
                        pltpu.VMEM((rows, WIDTH), F32), pltpu.VMEM((rows, 1), F32)])
    return pl.pallas_call(
        functools.partial(_sb_paged_body, tq=tq, group=group, n_steps=n_steps),
        grid_spec=grid_spec,
        out_shape=jax.ShapeDtypeStruct((bsz, tq, WIDTH), F32),
        compiler_params=_params(("parallel", "arbitrary")),
        name="sb_paged",
    )(page_table, bias, q, kt_new, vt_new, *([cache_kt] * group), *([cache_vt] * group))


def _mlstm_body(q_ref, k_ref, v_ref, og_ref, gc_ref, gr_ref, bc_ref, br_ref, hg_ref,
                c0_ref, n0_ref, m0_ref, y_ref, c_out_ref, n_out_ref, m_out_ref,
                c_s, n_s, m_s, *, chunk, n_valid, n_chunks):
    ci = pl.program_id(1)

    @pl.when(ci == 0)
    def _():
        c_s[...] = c0_ref[0]
        n_s[...] = n0_ref[0]
        m_s[...] = m0_ref[0]

    L = chunk
    r_i = lax.broadcasted_iota(jnp.int32, (L, L), 0)
    c_i = lax.broadcasted_iota(jnp.int32, (L, L), 1)
    causal = r_i >= c_i
    tri = jnp.where(causal, 1.0, 0.0).astype(F32)
    tri_t = jnp.where(r_i <= c_i, 1.0, 0.0).astype(F32)

    gate_c = gc_ref[0] + bc_ref[...]
    gate_r = gr_ref[0][:2 * HEADS] + br_ref[...]
    lane_g = lax.broadcasted_iota(jnp.int32, (L, GATE_LANES), 1)
    row_g = lax.broadcasted_iota(jnp.int32, (2 * HEADS, L), 0)
    x_c = jnp.where(lane_g >= HEADS, _log_sigmoid(gate_c), gate_c)
    x_r = jnp.where(row_g >= HEADS, _log_sigmoid(gate_r), gate_r)
    if n_valid < L:
        tok_c = lax.broadcasted_iota(jnp.int32, (L, GATE_LANES), 0)
        tok_r = lax.broadcasted_iota(jnp.int32, (2 * HEADS, L), 1)
        x_c = jnp.where(tok_c < n_valid, x_c, jnp.where(lane_g >= HEADS, 0.0, NEG_BIG))
        x_r = jnp.where(tok_r < n_valid, x_r, jnp.where(row_g >= HEADS, 0.0, NEG_BIG))
    cum_c = jnp.dot(tri, x_c, preferred_element_type=F32, precision=lax.Precision.HIGHEST)
    cum_r = jnp.dot(x_r, tri_t, preferred_element_type=F32, precision=lax.Precision.HIGHEST)

    lane = lax.broadcasted_iota(jnp.int32, (L, PAIR), 1)
    lane1 = lax.broadcasted_iota(jnp.int32, (1, PAIR), 1)
    blk_r = lax.broadcasted_iota(jnp.int32, (PAIR, PAIR), 0)
    blk_c = lax.broadcasted_iota(jnp.int32, (PAIR, PAIR), 1)
    same_head = (blk_r < HEAD_DIM) == (blk_c < HEAD_DIM)
    lane_m = lax.broadcasted_iota(jnp.int32, (1, HEADS), 1)
    m_old = m_s[...]
    m_next = m_old

    for p in range(N_PAIRS):
        cols = slice(p * PAIR, (p + 1) * PAIR)
        q2 = q_ref[0, :, cols]
        k2 = k_ref[0, :, cols]
        v2b = v_ref[0, :, cols].astype(BF16)
        k2b = k2.astype(BF16)
        c_pair = c_s[p]
        c_pair_b = c_pair.astype(BF16)
        n_row = n_s[p]
        h_pair = None
        wend_pair = None
        decay_pair = None
        for hh in range(2):
            h = 2 * p + hh
            mine = (lane >= HEAD_DIM) if hh else (lane < HEAD_DIM)
            b_c = cum_c[:, HEADS + h:HEADS + h + 1]
            i_c = x_c[:, h:h + 1]
            b_r = cum_r[HEADS + h:HEADS + h + 1, :]
            i_r = x_r[h:h + 1, :]
            m_prev = m_old[:, h:h + 1]
            log_d = jnp.where(causal, b_c - b_r + i_r, NEG_BIG)
            inter = b_c + m_prev
            m_t = jnp.maximum(inter, jnp.max(log_d, axis=1, keepdims=True))
            w_intra = jnp.exp(log_d - m_t)
            w_inter = jnp.exp(inter - m_t)
            qm = jnp.where(mine, q2, 0.0)
            qmb = qm.astype(BF16)
            s = lax.dot_general(qmb, k2b, _NT, preferred_element_type=F32) * w_intra
            num = (jnp.dot(s.astype(BF16), v2b, preferred_element_type=F32)
                   + w_inter * jnp.dot(qmb, c_pair_b, preferred_element_type=F32))
            den = (jnp.sum(s, axis=1, keepdims=True)
                   + w_inter * jnp.sum(qm * n_row, axis=1, keepdims=True))
            hid = num * (1.0 / jnp.maximum(jnp.abs(den), jnp.exp(-m_t)))
            ms = jnp.sum(jnp.where(mine, hid * hid, 0.0), axis=1, keepdims=True) * (1.0 / HEAD_DIM)
            hid = hid * lax.rsqrt(ms + RMS_EPS)
            b_last = b_r[:, L - 1:L]
            g_r = b_last - b_r + i_r
            m_new = jnp.maximum(b_last + m_prev, jnp.max(g_r, axis=1, keepdims=True))
            wend = jnp.exp(b_last - b_c + i_c - m_new)
            decay = jnp.exp(b_last + m_prev - m_new)
            m_next = jnp.where(lane_m == h, m_new, m_next)
            if hh == 0:
                h_pair, wend_pair, decay_pair = hid, wend, decay
            else:
                h_pair = jnp.where(mine, hid, h_pair)
                wend_pair = jnp.where(mine, wend, wend_pair)
                decay_pair = jnp.where(lane1 >= HEAD_DIM, decay, decay_pair)
        y_ref[0, :, cols] = (og_ref[0, :, cols] * h_pair * hg_ref[:, cols]).astype(BF16)
        kw = k2 * wend_pair
        upd = jnp.dot(kw.T.astype(BF16), v2b, preferred_element_type=F32)
        c_s[p] = decay_pair * c_pair + jnp.where(same_head, upd, 0.0)
        n_s[p] = decay_pair * n_row + jnp.sum(kw, axis=0, keepdims=True)
    m_s[...] = m_next

    @pl.when(ci == n_chunks - 1)
    def _():
        c_out_ref[0] = c_s[...]
        n_out_ref[0] = n_s[...]
        m_out_ref[0] = m_s[...]


def _pack_pairs(c):
    bsz = c.shape[0]
    c = c.reshape(bsz, N_PAIRS, 2, HEAD_DIM, HEAD_DIM)
    z = jnp.zeros_like(c[:, :, 0])
    top = jnp.concatenate([c[:, :, 0], z], axis=-1)
    bot = jnp.concatenate([z, c[:, :, 1]], axis=-1)
    return jnp.concatenate([top, bot], axis=-2)


def _unpack_pairs(cp):
    bsz = cp.shape[0]
    a = cp[:, :, :HEAD_DIM, :HEAD_DIM]
    b = cp[:, :, HEAD_DIM:, HEAD_DIM:]
    return jnp.stack([a, b], axis=2).reshape(bsz, HEADS, HEAD_DIM, HEAD_DIM)


def _mlstm(q, k, v, og, gates, bias_c, bias_r, head_gain, c0, n0, m0, n_valid):
    bsz, t, _ = q.shape
    L = ML_CHUNK
    n_chunks = t // L
    gates_r = jnp.swapaxes(gates, 1, 2)
    tok = lambda b, c: (b, c, 0)
    per_b3 = lambda b, c: (b, 0, 0)
    per_b4 = lambda b, c: (b, 0, 0, 0)
    y, c_out, n_out, m_out = pl.pallas_call(
        functools.partial(_mlstm_body, chunk=L, n_valid=n_valid, n_chunks=n_chunks),
        grid=(bsz, n_chunks),
        in_specs=[pl.BlockSpec((1, L, WIDTH), tok)] * 4 + [
            pl.BlockSpec((1, L, GATE_LANES), tok),
            pl.BlockSpec((1, GATE_LANES, L), lambda b, c: (b, 0, c)),
            pl.BlockSpec((1, GATE_LANES), lambda b, c: (0, 0)),
            pl.BlockSpec((2 * HEADS, 1), lambda b, c: (0, 0)),
            pl.BlockSpec((1, WIDTH), lambda b, c: (0, 0)),
            pl.BlockSpec((1, N_PAIRS, PAIR, PAIR), per_b4),
            pl.BlockSpec((1, N_PAIRS, 1, PAIR), per_b4),
            pl.BlockSpec((1, 1, HEADS), per_b3)],
        out_specs=[pl.BlockSpec((1, L, WIDTH), tok),
                   pl.BlockSpec((1, N_PAIRS, PAIR, PAIR), per_b4),
                   pl.BlockSpec((1, N_PAIRS, 1, PAIR), per_b4),
                   pl.BlockSpec((1, 1, HEADS), per_b3)],
        out_shape=[jax.ShapeDtypeStruct((bsz, t, WIDTH), BF16),
                   jax.ShapeDtypeStruct((bsz, N_PAIRS, PAIR, PAIR), F32),
                   jax.ShapeDtypeStruct((bsz, N_PAIRS, 1, PAIR), F32),
                   jax.ShapeDtypeStruct((bsz, 1, HEADS), F32)],
        scratch_shapes=[pltpu.VMEM((N_PAIRS, PAIR, PAIR), F32), pltpu.VMEM((N_PAIRS, 1, PAIR), F32),
                        pltpu.VMEM((1, HEADS), F32)],
        compiler_params=_params(("parallel", "arbitrary")),
        name="mlstm",
    )(q, k, v, og, gates, gates_r, bias_c, bias_r, head_gain,
      _pack_pairs(c0), n0.reshape(bsz, N_PAIRS, 1, PAIR), m0.reshape(bsz, 1, HEADS))
    return (y, _unpack_pairs(c_out), n_out.reshape(bsz, HEADS, HEAD_DIM), m_out.reshape(bsz, HEADS))


def _merge_body(x_ref, ysb_ref, yml_ref, cb_ref, u_ref, u1_ref, u2_ref, mg_ref,
                cw_ref, wbr_ref, wout_ref, o_ref):
    cw = cw_ref[...]
    conv = cw[0:1] * u2_ref[...] + cw[1:2] * u1_ref[...] + cw[2:3] * u_ref[...]
    ycv = (cb_ref[...] * conv).astype(BF16)
    merged = (mg_ref[:, :D_MODEL] * jnp.dot(ysb_ref[...], wbr_ref[0], preferred_element_type=F32)
              + mg_ref[:, D_MODEL:2 * D_MODEL] * jnp.dot(yml_ref[...], wbr_ref[1], preferred_element_type=F32)
              + mg_ref[:, 2 * D_MODEL:] * jnp.dot(ycv, wbr_ref[2], preferred_element_type=F32))
    o_ref[...] = x_ref[...] + jnp.dot(merged.astype(BF16), wout_ref[...], preferred_element_type=F32)


def _merge(x, ysb, yml, cb, u, u1, u2, mg, conv_w, w_br, w_out):
    m = x.shape[0]
    tm = _row_tile(m, ROW_TILE_PROJ)
    row = lambda i: (i, 0)
    wide = pl.BlockSpec((tm, WIDTH), row)
    return pl.pallas_call(
        _merge_body,
        grid=(m // tm,),
        in_specs=[pl.BlockSpec((tm, D_MODEL), row), wide, wide, wide, wide, wide, wide,
                  pl.BlockSpec((tm, 3 * D_MODEL), row),
                  _resident(conv_w.shape), _resident(w_br.shape), _resident(w_out.shape)],
        out_specs=pl.BlockSpec((tm, D_MODEL), row),
        out_shape=jax.ShapeDtypeStruct((m, D_MODEL), F32),
        compiler_params=_params(("parallel",)),
        name="merge",
    )(x, ysb, yml, cb, u, u1, u2, mg, conv_w, w_br, w_out)


def _shifted(u, buf):
    ext = jnp.concatenate([buf, u], axis=1)
    t = u.shape[1]
    return ext[:, 1:t + 1], ext[:, 0:t]


def _pad_tokens(a, t_to):
    return jnp.pad(a, ((0, 0), (0, t_to - a.shape[1]), (0, 0)))


def kernel(x_prompt, x_sample, cache_sb_k, cache_sb_v, page_table, state_ml_C, state_ml_n, state_ml_m,
           state_conv, meta_tokens, norm_ff1, ffn1_w_up, ffn1_w_down, norm_mix, w_in, sb_logit_bias,
           ml_igate_bias, ml_fgate_bias, ml_head_norm, conv_w, w_branch_sb, w_branch_ml, w_branch_cv, w_out,
           norm_ff2, ffn2_w_up, ffn2_w_down, norm_final):
    bp, seq, _ = x_prompt.shape
    bs, ts, _ = x_sample.shape
    depth = w_in.shape[0]
    d_ff = ffn1_w_down.shape[1]
    n_ff = d_ff // FF_CHUNK
    n_small = bs * ts
    blk = ATT_BLOCK
    assert seq % blk == 0 and seq % ML_CHUNK == 0 and d_ff % FF_CHUNK == 0
    assert ts % 8 == 0 and ts <= ML_CHUNK and N_META <= ML_CHUNK

    def ffn_weights(w_up, w_down):
        gate = w_up[:, :d_ff].reshape(D_MODEL, n_ff, FF_CHUNK)
        up = w_up[:, d_ff:].reshape(D_MODEL, n_ff, FF_CHUNK)
        wu = jnp.concatenate([gate, up], axis=-1).transpose(1, 0, 2).astype(BF16)
        return wu, w_down.reshape(n_ff, FF_CHUNK, D_MODEL).astype(BF16)

    def in_weights(w):
        main = jnp.concatenate([w[:, :6 * WIDTH], w[:, 6 * WIDTH + 2 * HEADS:]], axis=1)
        w_seg = main.reshape(D_MODEL, N_SEG, WIDTH).transpose(1, 0, 2).astype(BF16)
        gate = w[:, 6 * WIDTH:6 * WIDTH + 2 * HEADS]
        w_gate = jnp.pad(gate, ((0, 0), (0, GATE_LANES - 2 * HEADS))).astype(BF16)
        return w_seg, w_gate

    row1 = lambda a: a.reshape(1, -1)
    xb = x_prompt.reshape(bp * seq, D_MODEL)
    xs = jnp.concatenate([x_sample.reshape(n_small, D_MODEL), meta_tokens.astype(F32)], axis=0)
    pages_t = lambda c: jnp.transpose(c, (0, 1, 3, 4, 2)).reshape(c.shape[:2] + (WIDTH, PAGE_SIZE))
    cache_kt = pages_t(cache_sb_k)
    cache_vt = pages_t(cache_sb_v)

    new_p, new_s = [], []
    for l in range(depth):
        last = l == depth - 1
        wu1, wd1 = ffn_weights(ffn1_w_up[l], ffn1_w_down[l])
        wu2, wd2 = ffn_weights(ffn2_w_up[l], ffn2_w_down[l])
        w_seg, w_gate = in_weights(w_in[l])
        w_br = jnp.stack([w_branch_sb[l], w_branch_ml[l], w_branch_cv[l]]).astype(BF16)
        w_o = w_out[l].astype(BF16)
        bias_c = jnp.pad(jnp.concatenate([ml_igate_bias[l], ml_fgate_bias[l]]),
                         (0, GATE_LANES - 2 * HEADS)).reshape(1, GATE_LANES)
        bias_r = jnp.concatenate([ml_igate_bias[l], ml_fgate_bias[l]]).reshape(2 * HEADS, 1)
        head_gain = row1(ml_head_norm[l])

        xb = _ffn(xb, row1(norm_ff1[l]), wu1, wd1)
        xs = _ffn(xs, row1(norm_ff1[l]), wu1, wd1)
        pb = _in_proj(xb, row1(norm_mix[l]), w_seg, w_gate)
        ps = _in_proj(xs, row1(norm_mix[l]), w_seg, w_gate)
        (qa_b, kn_b, ka_b, vn_b, va_b, mq_b, mk_b, mv_b, og_b, cb_b, u_b, mg_b, ig_b) = pb
        (qa_s, kn_s, ka_s, vn_s, va_s, mq_s, mk_s, mv_s, og_s, cb_s, u_s, mg_s, ig_s) = ps
        seq3 = lambda a: a.reshape(bp, seq, a.shape[-1])
        smp3 = lambda a: a[:n_small].reshape(bs, ts, a.shape[-1])
        meta3 = lambda a: a[n_small:].reshape(1, N_META, a.shape[-1])
        meta_b = lambda a: jnp.broadcast_to(meta3(a), (bp, N_META, a.shape[-1]))

        def padded(a_small, a_big):
            lead = jnp.zeros((bp, blk - N_META, WIDTH), a_big.dtype)
            return jnp.concatenate([lead, meta_b(a_small), seq3(a_big)], axis=1)

        att_p = _sb_prompt(padded(qa_s, qa_b), padded(ka_s, ka_b), padded(va_s, va_b), sb_logit_bias[l], blk)
        ysb_b = att_p[:, blk:].reshape(bp * seq, WIDTH)
        ysb_meta = att_p[0, blk - N_META:blk]
        new_t = lambda a: jnp.pad(jnp.swapaxes(smp3(a), 1, 2), ((0, 0), (0, 0), (0, PAGE_SIZE - ts)))
        att_s = _sb_paged(smp3(qa_s).astype(F32), new_t(kn_s), new_t(vn_s), cache_kt, cache_vt, page_table,
                          sb_logit_bias[l], l)
        ysb_s = jnp.concatenate([att_s.reshape(n_small, WIDTH).astype(BF16), ysb_meta], axis=0)

        ml_args = (bias_c, bias_r, head_gain)
        zc = jnp.zeros((1, HEADS, HEAD_DIM, HEAD_DIM), F32)
        zn = jnp.zeros((1, HEADS, HEAD_DIM), F32)
        zm = jnp.zeros((1, HEADS), F32)
        padm = lambda a: _pad_tokens(meta3(a), ML_CHUNK)
        yml_meta, c_m, n_m, m_m = _mlstm(padm(mq_s), padm(mk_s), padm(mv_s), padm(og_s), padm(ig_s),
                                         *ml_args, zc, zn, zm, N_META)
        rep = lambda a: jnp.broadcast_to(a, (bp,) + a.shape[1:])
        yml_b, c_p, n_p, m_p = _mlstm(seq3(mq_b), seq3(mk_b), seq3(mv_b), seq3(og_b), seq3(ig_b),
                                      *ml_args, rep(c_m), rep(n_m), rep(m_m), ML_CHUNK)
        pads = lambda a: _pad_tokens(smp3(a), ML_CHUNK)
        yml_smp, c_s, n_s, m_s = _mlstm(pads(mq_s), pads(mk_s), pads(mv_s), pads(og_s), pads(ig_s),
                                        *ml_args, state_ml_C[l].astype(F32), state_ml_n[l].astype(F32),
                                        state_ml_m[l].astype(F32), ts)
        yml_s = jnp.concatenate([yml_smp[:, :ts].reshape(n_small, WIDTH), yml_meta[0, :N_META]], axis=0)

        u_meta = meta3(u_s)
        u1_meta, u2_meta = _shifted(u_meta, jnp.zeros((1, 2, WIDTH), F32))
        u_seq = seq3(u_b)
        u1_b, u2_b = _shifted(u_seq, rep(u_meta[:, -2:]))
        u_smp = smp3(u_s)
        u1_smp, u2_smp = _shifted(u_smp, state_conv[l].astype(F32))
        flat = lambda a: a.reshape(-1, WIDTH)
        u1_s = jnp.concatenate([flat(u1_smp), flat(u1_meta)], axis=0)
        u2_s = jnp.concatenate([flat(u2_smp), flat(u2_meta)], axis=0)

        xb = _merge(xb, ysb_b, yml_b.reshape(bp * seq, WIDTH), cb_b, u_b, flat(u1_b), flat(u2_b), mg_b,
                    conv_w[l], w_br, w_o)
        xs = _merge(xs, ysb_s, yml_s, cb_s, u_s, u1_s, u2_s, mg_s, conv_w[l], w_br, w_o)

        fin = row1(norm_final) if last else None
        xb = _ffn(xb, row1(norm_ff2[l]), wu2, wd2, fin)
        xs = _ffn(xs, row1(norm_ff2[l]), wu2, wd2, fin)

        heads = lambda a: a.reshape(a.shape[:-1] + (HEADS, HEAD_DIM))
        new_p.append((heads(jnp.concatenate([meta_b(kn_s), seq3(kn_b)], axis=1)),
                      heads(jnp.concatenate([meta_b(vn_s), seq3(vn_b)], axis=1)),
                      c_p, n_p, m_p, u_seq[:, -2:]))
        new_s.append((heads(smp3(kn_s)), heads(smp3(vn_s)), c_s, n_s, m_s,
                      jnp.concatenate([state_conv[l].astype(F32), u_smp], axis=1)[:, -2:]))

    y_prompt = xb.reshape(bp, seq, D_MODEL)
    y_sample = xs[:n_small].reshape(bs, ts, D_MODEL)
    stk = lambda states, i: jnp.stack([s[i] for s in states])
    return (y_prompt, y_sample) + tuple(stk(new_p, i) for i in range(6)) + tuple(stk(new_s, i) for i in range(6))
```

```python
import functools

import jax
import jax.numpy as jnp
from jax import lax
from jax.experimental import pallas as pl
from jax.experimental.pallas import tpu as pltpu

F32 = jnp.float32
BF16 = jnp.bfloat16

D_MODEL = 1024
N_META = 16
HEADS = 8
HEAD_DIM = 64
WIDTH = HEADS * HEAD_DIM
PAIR = 2 * HEAD_DIM
N_PAIRS = HEADS // 2
PAGE_SIZE = 128
RMS_EPS = 1e-6
QK_SCALE = HEAD_DIM ** -0.5
NEG_BIG = -1e30
LOG2E = 1.4426950408889634

FF_CHUNK = 256
ROW_TILE_FFN = 512
ROW_TILE_PROJ = 256
ATT_BLOCK = 256
ATT_PAIRS_PER_STEP = 4
ML_CHUNK = 128
ML_SEQS_PER_STEP = 2
PAGE_GROUP = 8
N_SEG = 16
GATE_LANES = 128
VMEM_LIMIT = 56 * 1024 * 1024

_NT = (((1,), (1,)), ((), ()))


def _params(semantics):
    return pltpu.CompilerParams(dimension_semantics=semantics, vmem_limit_bytes=VMEM_LIMIT)


def _resident(shape):
    nd = len(shape)
    return pl.BlockSpec(shape, lambda *_: (0,) * nd, pipeline_mode=pl.Buffered(1))


def _rms(x, gain):
    return x * lax.rsqrt(jnp.mean(x * x, axis=-1, keepdims=True) + RMS_EPS) * gain


def _log_sigmoid(x):
    return jnp.minimum(x, 0.0) - jnp.log1p(jnp.exp(-jnp.abs(x)))


def _row_tile(m, target):
    for t in (target, target // 2, target // 4):
        if m % t == 0:
            return t
    return m


def _ffn_body(*refs, n_chunks, final):
    if final:
        x_ref, g_ref, wup_ref, wdn_ref, gf_ref, o_ref, xn_ref, acc_ref = refs
    else:
        x_ref, g_ref, wup_ref, wdn_ref, o_ref, xn_ref, acc_ref = refs
    xn_ref[...] = _rms(x_ref[...], g_ref[...]).astype(BF16)
    acc_ref[...] = jnp.zeros_like(acc_ref)

    def chunk(c, carry):
        gu = jnp.dot(xn_ref[...], wup_ref[c], preferred_element_type=F32)
        gate, up = gu[:, :FF_CHUNK], gu[:, FF_CHUNK:]
        act = (gate * jax.nn.sigmoid(gate) * up).astype(BF16)
        acc_ref[...] += jnp.dot(act, wdn_ref[c], preferred_element_type=F32)
        return carry

    lax.fori_loop(0, n_chunks, chunk, 0, unroll=True)
    y = x_ref[...] + 0.5 * acc_ref[...]
    if final:
        y = _rms(y, gf_ref[...])
    o_ref[...] = y


def _ffn(x, gain, w_up, w_down, final_gain=None):
    m = x.shape[0]
    tm = _row_tile(m, ROW_TILE_FFN)
    n_chunks = w_up.shape[0]
    final = final_gain is not None
    row = lambda i: (i, 0)
    in_specs = [pl.BlockSpec((tm, D_MODEL), row), _resident((1, D_MODEL)),
                _resident(w_up.shape), _resident(w_down.shape)]
    args = [x, gain, w_up, w_down]
    if final:
        in_specs.append(_resident((1, D_MODEL)))
        args.append(final_gain)
    return pl.pallas_call(
        functools.partial(_ffn_body, n_chunks=n_chunks, final=final),
        grid=(m // tm,),
        in_specs=in_specs,
        out_specs=pl.BlockSpec((tm, D_MODEL), row),
        out_shape=jax.ShapeDtypeStruct((m, D_MODEL), F32),
        scratch_shapes=[pltpu.VMEM((tm, D_MODEL), BF16), pltpu.VMEM((tm, D_MODEL), F32)],
        compiler_params=_params(("parallel",)),
        name="ffn",
    )(*args)


def _inproj_body(x_ref, g_ref, wseg_ref, wgate_ref,
                 qa_ref, kn_ref, ka_ref, vn_ref, va_ref, mq_ref, mk_ref, mv_ref,
                 mo_ref, cb_ref, u_ref, mg_ref, ig_ref, xn_ref):
    xn_ref[...] = _rms(x_ref[...], g_ref[...]).astype(BF16)
    seg = lambda s: jnp.dot(xn_ref[...], wseg_ref[s], preferred_element_type=F32)
    qa_ref[...] = (seg(0) * QK_SCALE).astype(BF16)
    k = seg(1)
    kn_ref[...] = k
    ka_ref[...] = k.astype(BF16)
    v = seg(2)
    vn_ref[...] = v
    va_ref[...] = v.astype(BF16)
    mq_ref[...] = seg(3)
    mk_ref[...] = seg(4) * QK_SCALE
    mv_ref[...] = seg(5)
    mo_ref[...] = jax.nn.sigmoid(seg(6))
    cb_ref[...] = seg(7)
    u_ref[...] = seg(8) * seg(9)
    for s in range(6):
        mg_ref[:, s * WIDTH:(s + 1) * WIDTH] = jax.nn.sigmoid(seg(10 + s))
    ig_ref[...] = jnp.dot(xn_ref[...], wgate_ref[...], preferred_element_type=F32)


def _in_proj(x, gain, w_seg, w_gate):
    m = x.shape[0]
    tm = _row_tile(m, ROW_TILE_PROJ)
    row = lambda i: (i, 0)
    wide = lambda dt: jax.ShapeDtypeStruct((m, WIDTH), dt)
    out_shape = [wide(BF16), wide(F32), wide(BF16), wide(F32), wide(BF16),
                 wide(F32), wide(F32), wide(F32), wide(F32), wide(F32), wide(F32),
                 jax.ShapeDtypeStruct((m, 3 * D_MODEL), F32),
                 jax.ShapeDtypeStruct((m, GATE_LANES), F32)]
    out_specs = [pl.BlockSpec((tm, WIDTH), row)] * 11 + [
        pl.BlockSpec((tm, 3 * D_MODEL), row), pl.BlockSpec((tm, GATE_LANES), row)]
    return pl.pallas_call(
        _inproj_body,
        grid=(m // tm,),
        in_specs=[pl.BlockSpec((tm, D_MODEL), row), _resident((1, D_MODEL)),
                  _resident(w_seg.shape), _resident(w_gate.shape)],
        out_specs=out_specs,
        out_shape=out_shape,
        scratch_shapes=[pltpu.VMEM((tm, D_MODEL), BF16)],
        compiler_params=_params(("parallel",)),
        name="in_proj",
    )(x, gain, w_seg, w_gate)


def _strict_upper(blk):
    r = lax.broadcasted_iota(jnp.int32, (blk, blk), 0)
    c = lax.broadcasted_iota(jnp.int32, (blk, blk), 1)
    return jnp.where(r > c, 1.0, 0.0).astype(BF16)


def _sb_logs(z, valid):
    soft = jnp.log(1.0 + jnp.exp2(jnp.abs(z) * (-LOG2E)))
    log_beta = jnp.minimum(z, 0.0) - soft
    log_keep = log_beta - z
    if valid is not None:
        log_keep = jnp.where(valid, log_keep, 0.0)
    return log_beta, log_keep.astype(BF16)


def _sb_rest(log_keep, upper):
    rest = jnp.dot(log_keep, upper, preferred_element_type=F32)
    return rest, rest[:, :1] + log_keep[:, :1].astype(F32)


def _sb_local_weights(log_beta, rest, valid):
    w = jnp.exp(log_beta + rest)
    if valid is not None:
        w = jnp.where(valid, w, 0.0)
    return w.astype(BF16)


def _sb_prompt_body(bias_ref, q_ref, k_ref, v_ref, o_ref, qs_ref, uu_ref, acc_ref, carry_ref, *, blk, n_par):
    first_pair = pl.program_id(1) * n_par
    i = pl.program_id(2)
    lane = lax.broadcasted_iota(jnp.int32, (blk, PAIR), 1)
    for a in range(n_par):
        q2 = q_ref[0, :, a * PAIR:(a + 1) * PAIR]
        zero = jnp.zeros_like(q2)
        qs_ref[a, :blk] = jnp.where(lane < HEAD_DIM, q2, zero)
        qs_ref[a, blk:] = jnp.where(lane >= HEAD_DIM, q2, zero)
    uu_ref[...] = _strict_upper(blk)
    acc_ref[...] = jnp.zeros_like(acc_ref)
    carry_ref[...] = jnp.zeros_like(carry_ref)
    bias = [(bias_ref[2 * (first_pair + a)], bias_ref[2 * (first_pair + a) + 1]) for a in range(n_par)]

    def step(kb, masked):
        start = pl.multiple_of(kb * blk, blk)
        pairs = range(n_par)
        cols = lambda a: slice(a * PAIR, (a + 1) * PAIR)
        s = [lax.dot_general(qs_ref[a], k_ref[0, pl.ds(start, blk), cols(a)], _NT, preferred_element_type=F32)
             for a in pairs]
        z = [jnp.concatenate([s[a][:blk] + bias[a][0], s[a][blk:] + bias[a][1]], axis=0) for a in pairs]
        valid = None
        if masked:
            row = lax.broadcasted_iota(jnp.int32, (2 * blk, blk), 0)
            col = lax.broadcasted_iota(jnp.int32, (2 * blk, blk), 1)
            valid = col < jnp.where(row >= blk, row - blk, row)
        logs = [_sb_logs(z[a], valid) for a in pairs]
        rest = [_sb_rest(logs[a][1], uu_ref[...]) for a in pairs]
        w = [_sb_local_weights(logs[a][0], rest[a][0], valid) for a in pairs]
        for a in pairs:
            carry = carry_ref[a]
            pv = jnp.dot(w[a], v_ref[0, pl.ds(start, blk), cols(a)], preferred_element_type=F32)
            acc_ref[a] += jnp.exp(carry) * pv
            carry_ref[a] = carry + rest[a][1]

    step(i, True)

    def block(jb, carry):
        step(i - jb, False)
        return carry

    lax.fori_loop(1, i + 1, block, 0)
    for a in range(n_par):
        o_ref[0, :, a * PAIR:(a + 1) * PAIR] = jnp.where(
            lane < HEAD_DIM, acc_ref[a, :blk], acc_ref[a, blk:]).astype(BF16)


def _sb_prompt(q, k, v, bias, blk):
    bsz, tp, _ = q.shape
    nq = tp // blk
    n_par = ATT_PAIRS_PER_STEP
    wide = n_par * PAIR
    return pl.pallas_call(
        functools.partial(_sb_prompt_body, blk=blk, n_par=n_par),
        grid=(bsz, N_PAIRS // n_par, nq),
        in_specs=[pl.BlockSpec(memory_space=pltpu.SMEM),
                  pl.BlockSpec((1, blk, wide), lambda b, p, i: (b, i, p)),
                  pl.BlockSpec((1, tp, wide), lambda b, p, i: (b, 0, p)),
                  pl.BlockSpec((1, tp, wide), lambda b, p, i: (b, 0, p))],
        out_specs=pl.BlockSpec((1, blk, wide), lambda b, p, i: (b, i, p)),
        out_shape=jax.ShapeDtypeStruct((bsz, tp, WIDTH), BF16),
        scratch_shapes=[pltpu.VMEM((n_par, 2 * blk, PAIR), BF16), pltpu.VMEM((blk, blk), BF16),
                        pltpu.VMEM((n_par, 2 * blk, PAIR), F32), pltpu.VMEM((n_par, 2 * blk, 1), F32)],
        compiler_params=_params(("parallel", "parallel", "arbitrary")),
        name="sb_prompt",
    )(bias, q, k, v)


def _sb_paged_body(pt_ref, bias_ref, q_ref, kn_ref, vn_ref, *rest, tq, group, n_steps):
    del pt_ref
    k_refs = rest[:group]
    v_refs = rest[group:2 * group]
    o_ref, qbd_ref, bias_t_ref, uu_ref, acc_ref, carry_ref = rest[2 * group:]
    j = pl.program_id(1)
    rows = HEADS * tq

    def sweep(kts, vts, carry, acc, valid):
        n = len(kts)
        kt = jnp.concatenate([t.astype(BF16) for t in kts], axis=1)
        vt = jnp.concatenate([t.astype(BF16) for t in vts], axis=1)
        s = jnp.dot(qbd_ref[...], kt, preferred_element_type=F32)
        page = lambda a, g: a[:, g * PAGE_SIZE:(g + 1) * PAGE_SIZE]
        logs = [_sb_logs(page(s, g) + bias_t_ref[...], valid) for g in range(n)]
        rest, total = _sb_rest(jnp.concatenate([lg[1] for lg in logs], axis=0), uu_ref[...])
        w = []
        for g in range(n):
            rows_g = slice(g * rows, (g + 1) * rows)
            w.append(_sb_local_weights(logs[g][0] + carry, rest[rows_g], valid))
            carry = carry + total[rows_g]
        w = jnp.concatenate(w, axis=1)
        return carry, acc + lax.dot_general(w, vt, _NT, preferred_element_type=F32)

    @pl.when(j == 0)
    def _():
        rw = lax.broadcasted_iota(jnp.int32, (rows, WIDTH), 0)
        cw = lax.broadcasted_iota(jnp.int32, (rows, WIDTH), 1)
        q_all = jnp.concatenate([q_ref[0]] * HEADS, axis=0)
        qbd_ref[...] = jnp.where(rw // tq == cw // HEAD_DIM, q_all, 0.0).astype(BF16)
        row = lax.broadcasted_iota(jnp.int32, (rows, PAGE_SIZE), 0)
        col = lax.broadcasted_iota(jnp.int32, (rows, PAGE_SIZE), 1)
        bias_t = jnp.zeros((rows, PAGE_SIZE), F32)
        for h in range(HEADS):
            bias_t = jnp.where(row // tq == h, bias_ref[h], bias_t)
        bias_t_ref[...] = bias_t
        uu_ref[...] = _strict_upper(PAGE_SIZE)
        valid = col < row % tq
        carry, acc = sweep([kn_ref[0]], [vn_ref[0]], jnp.zeros((rows, 1), F32), jnp.zeros((rows, WIDTH), F32), valid)
        carry_ref[...] = carry
        acc_ref[...] = acc

    carry, acc = sweep([r[0, 0] for r in k_refs], [r[0, 0] for r in v_refs], carry_ref[...], acc_ref[...], None)
    carry_ref[...] = carry
    acc_ref[...] = acc

    @pl.when(j == n_steps - 1)
    def _():
        rw = lax.broadcasted_iota(jnp.int32, (tq, WIDTH), 1)
        out = jnp.zeros((tq, WIDTH), F32)
        for h in range(HEADS):
            out = jnp.where(rw // HEAD_DIM == h, acc_ref[h * tq:(h + 1) * tq], out)
        o_ref[0] = out


def _sb_paged(q, kt_new, vt_new, cache_kt, cache_vt, page_table, bias, layer):
    bsz, tq, _ = q.shape
    n_pages = page_table.shape[1]
    group = next(g for g in (PAGE_GROUP, 4, 2, 1) if n_pages % g == 0)
    n_steps = n_pages // group
    rows = HEADS * tq
    per_b = lambda b, j, pt: (b, 0, 0)

    def page(g):
        return lambda b, j, pt: (layer, pt[b, n_pages - 1 - (j * group + g)], 0, 0)

    page_specs = [pl.BlockSpec((1, 1, WIDTH, PAGE_SIZE), page(g)) for g in range(group)]
    grid_spec = pltpu.PrefetchScalarGridSpec(
        num_scalar_prefetch=1,
        grid=(bsz, n_steps),
        in_specs=[pl.BlockSpec(memory_space=pltpu.SMEM),
                  pl.BlockSpec((1, tq, WIDTH), per_b),
                  pl.BlockSpec((1, WIDTH, PAGE_SIZE), per_b),
                  pl.BlockSpec((1, WIDTH, PAGE_SIZE), per_b)] + page_specs + page_specs,
        out_specs=pl.BlockSpec((1, tq, WIDTH), per_b),
        scratch_shapes=[pltpu.VMEM((rows, WIDTH), BF16), pltpu.VMEM((rows, PAGE_SIZE), F32),
                        pltpu.VMEM((PAGE_SIZE, PAGE_SIZE), BF16),
                        pltpu.VMEM((rows, WIDTH), F32), pltpu.VMEM((rows, 1), F32)])
    return pl.pallas_call(
        functools.partial(_sb_paged_body, tq=tq, group=group, n_steps=n_steps),
        grid_spec=grid_spec,
        out_shape=jax.ShapeDtypeStruct((bsz, tq, WIDTH), F32),
        compiler_params=_params(("parallel", "arbitrary")),
        name="sb_paged",
    )(page_table, bias, q, kt_new, vt_new, *([cache_kt] * group), *([cache_vt] * group))


def _mlstm_body(q_ref, k_ref, v_ref, og_ref, gc_ref, bc_ref, hg_ref, c0_ref, n0_ref, m0_ref,
                y_ref, c_out_ref, n_out_ref, m_out_ref, c_s, n_s, m_s, *, chunk, n_seq, n_tok, n_real, n_chunks):
    ci = pl.program_id(1)

    @pl.when(ci == 0)
    def _():
        c_s[...] = c0_ref[...]
        n_s[...] = n0_ref[...]
        m_s[...] = m0_ref[...]

    L = chunk
    short = n_tok < L

    def tokens(ref, b, cols=slice(None)):
        x = ref[b, :, cols]
        if short:
            x = jnp.concatenate([x, jnp.zeros((L - n_tok, x.shape[1]), x.dtype)], axis=0)
        return x

    r_i = lax.broadcasted_iota(jnp.int32, (L, L), 0)
    c_i = lax.broadcasted_iota(jnp.int32, (L, L), 1)
    causal = r_i >= c_i
    tri = jnp.where(causal, 1.0, 0.0).astype(F32)
    lane_g = lax.broadcasted_iota(jnp.int32, (L, GATE_LANES), 1)
    lane = lax.broadcasted_iota(jnp.int32, (L, PAIR), 1)
    lane1 = lax.broadcasted_iota(jnp.int32, (1, PAIR), 1)
    blk_r = lax.broadcasted_iota(jnp.int32, (PAIR, PAIR), 0)
    blk_c = lax.broadcasted_iota(jnp.int32, (PAIR, PAIR), 1)
    same_head = (blk_r < HEAD_DIM) == (blk_c < HEAD_DIM)
    lane_m = lax.broadcasted_iota(jnp.int32, (1, HEADS), 1)
    mine = [lane < HEAD_DIM, lane >= HEAD_DIM]
    cols = [slice(p * PAIR, (p + 1) * PAIR) for p in range(N_PAIRS)]
    seqs = range(n_seq)
    probs = [(b, h) for b in seqs for h in range(HEADS)]
    pair_ids = [(b, p) for b in seqs for p in range(N_PAIRS)]

    x_c, cum_c, x_r, cum_r = {}, {}, {}, {}
    for b in seqs:
        gate_c = tokens(gc_ref, b) + bc_ref[...]
        xc = jnp.where(lane_g >= HEADS, _log_sigmoid(gate_c), gate_c)
        if n_real < n_chunks * L:
            tok_c = ci * L + lax.broadcasted_iota(jnp.int32, (L, GATE_LANES), 0)
            xc = jnp.where(tok_c < n_real, xc, jnp.where(lane_g >= HEADS, 0.0, NEG_BIG))
        cc = jnp.dot(tri, xc, preferred_element_type=F32, precision=lax.Precision.HIGHEST)
        x_c[b], cum_c[b] = xc, cc
        x_r[b] = xc.T[:2 * HEADS]
        cum_r[b] = cc.T[:2 * HEADS]

    q2 = {bp: tokens(q_ref, bp[0], cols[bp[1]]) for bp in pair_ids}
    k2 = {bp: tokens(k_ref, bp[0], cols[bp[1]]) for bp in pair_ids}
    k2b = {bp: k2[bp].astype(BF16) for bp in pair_ids}
    v2b = {bp: tokens(v_ref, bp[0], cols[bp[1]]).astype(BF16) for bp in pair_ids}
    c_pair = {bp: c_s[bp[0], bp[1]] for bp in pair_ids}
    n_row = {bp: n_s[bp[0], bp[1]] for bp in pair_ids}
    pair_of = lambda bh: (bh[0], bh[1] // 2)
    qm = {bh: jnp.where(mine[bh[1] % 2], q2[pair_of(bh)], 0.0) for bh in probs}
    qmb = {bh: qm[bh].astype(BF16) for bh in probs}

    b_c, i_c, b_r, i_r, m_prev, m_new, wend, decay = {}, {}, {}, {}, {}, {}, {}, {}
    for bh in probs:
        b, h = bh
        b_c[bh] = cum_c[b][:, HEADS + h:HEADS + h + 1]
        i_c[bh] = x_c[b][:, h:h + 1]
        b_r[bh] = cum_r[b][HEADS + h:HEADS + h + 1, :]
        i_r[bh] = x_r[b][h:h + 1, :]
        m_prev[bh] = m_s[b][:, h:h + 1]
        b_last = b_r[bh][:, L - 1:L]
        m_new[bh] = jnp.maximum(b_last + m_prev[bh],
                                jnp.max(b_last - b_r[bh] + i_r[bh], axis=1, keepdims=True))
        wend[bh] = jnp.exp(b_last - b_c[bh] + i_c[bh] - m_new[bh])
        decay[bh] = jnp.exp(b_last + m_prev[bh] - m_new[bh])
    kw = {(b, p): k2[(b, p)] * jnp.where(mine[1], wend[(b, 2 * p + 1)], wend[(b, 2 * p)]) for b, p in pair_ids}

    qk = {bh: lax.dot_general(qmb[bh], k2b[pair_of(bh)], _NT, preferred_element_type=F32) for bh in probs}
    qc = {bh: jnp.dot(qmb[bh], c_pair[pair_of(bh)].astype(BF16), preferred_element_type=F32) for bh in probs}
    upd = {bp: jnp.dot(kw[bp].T.astype(BF16), v2b[bp], preferred_element_type=F32) for bp in pair_ids}

    s, w_inter, m_t = {}, {}, {}
    for bh in probs:
        log_d = jnp.where(causal, b_c[bh] - b_r[bh] + i_r[bh], NEG_BIG)
        inter = b_c[bh] + m_prev[bh]
        m_t[bh] = jnp.maximum(inter, jnp.max(log_d, axis=1, keepdims=True))
        s[bh] = qk[bh] * jnp.exp(log_d - m_t[bh])
        w_inter[bh] = jnp.exp(inter - m_t[bh])
    sv = {bh: jnp.dot(s[bh].astype(BF16), v2b[pair_of(bh)], preferred_element_type=F32) for bh in probs}

    hid = {}
    for bh in probs:
        num = sv[bh] + w_inter[bh] * qc[bh]
        den = (jnp.sum(s[bh], axis=1, keepdims=True)
               + w_inter[bh] * jnp.sum(qm[bh] * n_row[pair_of(bh)], axis=1, keepdims=True))
        x = num * (1.0 / jnp.maximum(jnp.abs(den), jnp.exp(-m_t[bh])))
        ms = jnp.sum(jnp.where(mine[bh[1] % 2], x * x, 0.0), axis=1, keepdims=True) * (1.0 / HEAD_DIM)
        hid[bh] = x * lax.rsqrt(ms + RMS_EPS)

    for b in seqs:
        m_next = m_s[b]
        for h in range(HEADS):
            m_next = jnp.where(lane_m == h, m_new[(b, h)], m_next)
        m_s[b] = m_next
    for b, p in pair_ids:
        h_pair = jnp.where(mine[1], hid[(b, 2 * p + 1)], hid[(b, 2 * p)])
        y = tokens(og_ref, b, cols[p]) * h_pair * hg_ref[:, cols[p]]
        y_ref[b, :, cols[p]] = y[:n_tok].astype(y_ref.dtype) if short else y.astype(y_ref.dtype)
        decay_pair = jnp.where(lane1 >= HEAD_DIM, decay[(b, 2 * p + 1)], decay[(b, 2 * p)])
        c_s[b, p] = decay_pair * c_pair[(b, p)] + jnp.where(same_head, upd[(b, p)], 0.0)
        n_s[b, p] = decay_pair * n_row[(b, p)] + jnp.sum(kw[(b, p)], axis=0, keepdims=True)

    @pl.when(ci == n_chunks - 1)
    def _():
        c_out_ref[...] = c_s[...]
        n_out_ref[...] = n_s[...]
        m_out_ref[...] = m_s[...]


def _pack_pairs(c):
    bsz = c.shape[0]
    c = c.reshape(bsz, N_PAIRS, 2, HEAD_DIM, HEAD_DIM)
    z = jnp.zeros_like(c[:, :, 0])
    top = jnp.concatenate([c[:, :, 0], z], axis=-1)
    bot = jnp.concatenate([z, c[:, :, 1]], axis=-1)
    return jnp.concatenate([top, bot], axis=-2)


def _unpack_pairs(cp):
    bsz = cp.shape[0]
    a = cp[:, :, :HEAD_DIM, :HEAD_DIM]
    b = cp[:, :, HEAD_DIM:, HEAD_DIM:]
    return jnp.stack([a, b], axis=2).reshape(bsz, HEADS, HEAD_DIM, HEAD_DIM)


def _mlstm(q, k, v, og, gates, bias_c, head_gain, c0, n0, m0, n_real):
    bsz, t, _ = q.shape
    L = ML_CHUNK
    n_chunks = max(t // L, 1)
    tb = min(t, L)
    assert t == n_chunks * tb
    nb = ML_SEQS_PER_STEP if bsz % ML_SEQS_PER_STEP == 0 else 1
    y_dtype = BF16 if tb == L else F32
    tok = lambda b, c: (b, c, 0)
    per_b3 = lambda b, c: (b, 0, 0)
    per_b4 = lambda b, c: (b, 0, 0, 0)
    y, c_out, n_out, m_out = pl.pallas_call(
        functools.partial(_mlstm_body, chunk=L, n_seq=nb, n_tok=tb, n_real=n_real, n_chunks=n_chunks),
        grid=(bsz // nb, n_chunks),
        in_specs=[pl.BlockSpec((nb, tb, WIDTH), tok)] * 4 + [
            pl.BlockSpec((nb, tb, GATE_LANES), tok),
            pl.BlockSpec((1, GATE_LANES), lambda b, c: (0, 0)),
            pl.BlockSpec((1, WIDTH), lambda b, c: (0, 0)),
            pl.BlockSpec((nb, N_PAIRS, PAIR, PAIR), per_b4),
            pl.BlockSpec((nb, N_PAIRS, 1, PAIR), per_b4),
            pl.BlockSpec((nb, 1, HEADS), per_b3)],
        out_specs=[pl.BlockSpec((nb, tb, WIDTH), tok),
                   pl.BlockSpec((nb, N_PAIRS, PAIR, PAIR), per_b4),
                   pl.BlockSpec((nb, N_PAIRS, 1, PAIR), per_b4),
                   pl.BlockSpec((nb, 1, HEADS), per_b3)],
        out_shape=[jax.ShapeDtypeStruct((bsz, t, WIDTH), y_dtype),
                   jax.ShapeDtypeStruct((bsz, N_PAIRS, PAIR, PAIR), F32),
                   jax.ShapeDtypeStruct((bsz, N_PAIRS, 1, PAIR), F32),
                   jax.ShapeDtypeStruct((bsz, 1, HEADS), F32)],
        scratch_shapes=[pltpu.VMEM((nb, N_PAIRS, PAIR, PAIR), F32), pltpu.VMEM((nb, N_PAIRS, 1, PAIR), F32),
                        pltpu.VMEM((nb, 1, HEADS), F32)],
        compiler_params=_params(("parallel", "arbitrary")),
        name="mlstm",
    )(q, k, v, og, gates, bias_c, head_gain,
      _pack_pairs(c0), n0.reshape(bsz, N_PAIRS, 1, PAIR), m0.reshape(bsz, 1, HEADS))
    return (y, _unpack_pairs(c_out), n_out.reshape(bsz, HEADS, HEAD_DIM), m_out.reshape(bsz, HEADS))


def _merge_body(x_ref, ysb_ref, yml_ref, cb_ref, u_ref, u1_ref, u2_ref, mg_ref,
                cw_ref, wbr_ref, wout_ref, o_ref):
    cw = cw_ref[...]
    conv = cw[0:1] * u2_ref[...] + cw[1:2] * u1_ref[...] + cw[2:3] * u_ref[...]
    ycv = (cb_ref[...] * conv).astype(BF16)
    merged = (mg_ref[:, :D_MODEL] * jnp.dot(ysb_ref[...], wbr_ref[0], preferred_element_type=F32)
              + mg_ref[:, D_MODEL:2 * D_MODEL] * jnp.dot(yml_ref[...], wbr_ref[1], preferred_element_type=F32)
              + mg_ref[:, 2 * D_MODEL:] * jnp.dot(ycv, wbr_ref[2], preferred_element_type=F32))
    o_ref[...] = x_ref[...] + jnp.dot(merged.astype(BF16), wout_ref[...], preferred_element_type=F32)


def _merge(x, ysb, yml, cb, u, u1, u2, mg, conv_w, w_br, w_out):
    m = x.shape[0]
    tm = _row_tile(m, ROW_TILE_PROJ)
    row = lambda i: (i, 0)
    wide = pl.BlockSpec((tm, WIDTH), row)
    return pl.pallas_call(
        _merge_body,
        grid=(m // tm,),
        in_specs=[pl.BlockSpec((tm, D_MODEL), row), wide, wide, wide, wide, wide, wide,
                  pl.BlockSpec((tm, 3 * D_MODEL), row),
                  _resident(conv_w.shape), _resident(w_br.shape), _resident(w_out.shape)],
        out_specs=pl.BlockSpec((tm, D_MODEL), row),
        out_shape=jax.ShapeDtypeStruct((m, D_MODEL), F32),
        compiler_params=_params(("parallel",)),
        name="merge",
    )(x, ysb, yml, cb, u, u1, u2, mg, conv_w, w_br, w_out)


def _shifted(u, buf):
    ext = jnp.concatenate([buf, u], axis=1)
    t = u.shape[1]
    return ext[:, 1:t + 1], ext[:, 0:t]


def kernel(x_prompt, x_sample, cache_sb_k, cache_sb_v, page_table, state_ml_C, state_ml_n, state_ml_m,
           state_conv, meta_tokens, norm_ff1, ffn1_w_up, ffn1_w_down, norm_mix, w_in, sb_logit_bias,
           ml_igate_bias, ml_fgate_bias, ml_head_norm, conv_w, w_branch_sb, w_branch_ml, w_branch_cv, w_out,
           norm_ff2, ffn2_w_up, ffn2_w_down, norm_final):
    bp, seq, _ = x_prompt.shape
    bs, ts, _ = x_sample.shape
    depth = w_in.shape[0]
    d_ff = ffn1_w_down.shape[1]
    n_ff = d_ff // FF_CHUNK
    n_small = bs * ts
    blk = ATT_BLOCK
    nt = N_META + seq
    tf = -(-nt // blk) * blk
    assert blk % ML_CHUNK == 0 and d_ff % FF_CHUNK == 0
    assert ts % 8 == 0 and ts <= ML_CHUNK

    def ffn_weights(w_up, w_down):
        gate = w_up[:, :d_ff].reshape(D_MODEL, n_ff, FF_CHUNK)
        up = w_up[:, d_ff:].reshape(D_MODEL, n_ff, FF_CHUNK)
        wu = jnp.concatenate([gate, up], axis=-1).transpose(1, 0, 2).astype(BF16)
        return wu, w_down.reshape(n_ff, FF_CHUNK, D_MODEL).astype(BF16)

    def in_weights(w):
        main = jnp.concatenate([w[:, :6 * WIDTH], w[:, 6 * WIDTH + 2 * HEADS:]], axis=1)
        w_seg = main.reshape(D_MODEL, N_SEG, WIDTH).transpose(1, 0, 2).astype(BF16)
        gate = w[:, 6 * WIDTH:6 * WIDTH + 2 * HEADS]
        w_gate = jnp.pad(gate, ((0, 0), (0, GATE_LANES - 2 * HEADS))).astype(BF16)
        return w_seg, w_gate

    row1 = lambda a: a.reshape(1, -1)
    meta = jnp.broadcast_to(meta_tokens.astype(F32)[None], (bp, N_META, D_MODEL))
    xb = jnp.concatenate([meta, x_prompt, jnp.zeros((bp, tf - nt, D_MODEL), F32)], axis=1).reshape(bp * tf, D_MODEL)
    xs = x_sample.reshape(n_small, D_MODEL)
    pages_t = lambda c: jnp.transpose(c, (0, 1, 3, 4, 2)).reshape(c.shape[:2] + (WIDTH, PAGE_SIZE))
    cache_kt = pages_t(cache_sb_k)
    cache_vt = pages_t(cache_sb_v)
    frame = lambda a: a.reshape(bp, tf, a.shape[-1])
    smp = lambda a: a.reshape(bs, ts, a.shape[-1])
    flat = lambda a: a.reshape(-1, a.shape[-1])
    heads = lambda a: a.reshape(a.shape[:-1] + (HEADS, HEAD_DIM))

    new_p, new_s = [], []
    for l in range(depth):
        last = l == depth - 1
        wu1, wd1 = ffn_weights(ffn1_w_up[l], ffn1_w_down[l])
        wu2, wd2 = ffn_weights(ffn2_w_up[l], ffn2_w_down[l])
        w_seg, w_gate = in_weights(w_in[l])
        w_br = jnp.stack([w_branch_sb[l], w_branch_ml[l], w_branch_cv[l]]).astype(BF16)
        w_o = w_out[l].astype(BF16)
        bias_c = jnp.pad(jnp.concatenate([ml_igate_bias[l], ml_fgate_bias[l]]),
                         (0, GATE_LANES - 2 * HEADS)).reshape(1, GATE_LANES)
        ml_args = (bias_c, row1(ml_head_norm[l]))

        xb = _ffn(xb, row1(norm_ff1[l]), wu1, wd1)
        xs = _ffn(xs, row1(norm_ff1[l]), wu1, wd1)
        (qa_b, kn_b, ka_b, vn_b, va_b, mq_b, mk_b, mv_b, og_b, cb_b, u_b, mg_b, ig_b) = _in_proj(
            xb, row1(norm_mix[l]), w_seg, w_gate)
        (qa_s, kn_s, ka_s, vn_s, va_s, mq_s, mk_s, mv_s, og_s, cb_s, u_s, mg_s, ig_s) = _in_proj(
            xs, row1(norm_mix[l]), w_seg, w_gate)

        ysb_b = flat(_sb_prompt(frame(qa_b), frame(ka_b), frame(va_b), sb_logit_bias[l], blk))
        new_t = lambda a: jnp.pad(jnp.swapaxes(smp(a), 1, 2), ((0, 0), (0, 0), (0, PAGE_SIZE - ts)))
        ysb_s = flat(_sb_paged(smp(qa_s).astype(F32), new_t(kn_s), new_t(vn_s), cache_kt, cache_vt, page_table,
                               sb_logit_bias[l], l)).astype(BF16)

        yml_b, c_p, n_p, m_p = _mlstm(
            frame(mq_b), frame(mk_b), frame(mv_b), frame(og_b), frame(ig_b), *ml_args,
            jnp.zeros((bp, HEADS, HEAD_DIM, HEAD_DIM), F32), jnp.zeros((bp, HEADS, HEAD_DIM), F32),
            jnp.zeros((bp, HEADS), F32), nt)
        yml_s, c_s, n_s, m_s = _mlstm(
            smp(mq_s), smp(mk_s), smp(mv_s), smp(og_s), smp(ig_s), *ml_args,
            state_ml_C[l].astype(F32), state_ml_n[l].astype(F32), state_ml_m[l].astype(F32), ts)

        u_frame = frame(u_b)
        u1_b, u2_b = _shifted(u_frame, jnp.zeros((bp, 2, WIDTH), F32))
        u_smp = smp(u_s)
        u1_s, u2_s = _shifted(u_smp, state_conv[l].astype(F32))
        xb = _merge(xb, ysb_b, flat(yml_b), cb_b, u_b, flat(u1_b), flat(u2_b), mg_b, conv_w[l], w_br, w_o)
        xs = _merge(xs, ysb_s, flat(yml_s).astype(BF16), cb_s, u_s, flat(u1_s), flat(u2_s), mg_s,
                    conv_w[l], w_br, w_o)

        fin = row1(norm_final) if last else None
        xb = _ffn(xb, row1(norm_ff2[l]), wu2, wd2, fin)
        xs = _ffn(xs, row1(norm_ff2[l]), wu2, wd2, fin)

        new_p.append((heads(frame(kn_b)[:, :nt]), heads(frame(vn_b)[:, :nt]), c_p, n_p, m_p, u_frame[:, nt - 2:nt]))
        new_s.append((heads(smp(kn_s)), heads(smp(vn_s)), c_s, n_s, m_s,
                      jnp.concatenate([state_conv[l].astype(F32), u_smp], axis=1)[:, -2:]))

    y_prompt = frame(xb)[:, N_META:nt]
    y_sample = xs.reshape(bs, ts, D_MODEL)
    stk = lambda states, i: jnp.stack([st[i] for st in states])
    return (y_prompt, y_sample) + tuple(stk(new_p, i) for i in range(6)) + tuple(stk(new_s, i) for i in range(6))
```

```python
import functools

import jax
import jax.numpy as jnp
from jax import lax
from jax.experimental import pallas as pl
from jax.experimental.pallas import tpu as pltpu

F32 = jnp.float32
BF16 = jnp.bfloat16

D_MODEL = 1024
N_META = 16
HEADS = 8
HEAD_DIM = 64
WIDTH = HEADS * HEAD_DIM
PAIR = 2 * HEAD_DIM
N_PAIRS = HEADS // 2
PAGE_SIZE = 128
RMS_EPS = 1e-6
QK_SCALE = HEAD_DIM ** -0.5
NEG_BIG = -1e30
LOG2E = 1.4426950408889634

FF_CHUNK = 256
ROW_TILE_FFN = 512
ROW_TILE_PROJ = 256
ATT_BLOCK = 256
ATT_PAIRS_PER_STEP = 4
ML_CHUNK = 128
ML_SEQS_PER_STEP = 2
PAGE_GROUP = 16
N_SEG = 16
GATE_LANES = 128
VMEM_LIMIT = 56 * 1024 * 1024

_NT = (((1,), (1,)), ((), ()))


def _params(semantics):
    return pltpu.CompilerParams(dimension_semantics=semantics, vmem_limit_bytes=VMEM_LIMIT)


def _resident(shape):
    nd = len(shape)
    return pl.BlockSpec(shape, lambda *_: (0,) * nd, pipeline_mode=pl.Buffered(1))


def _rms(x, gain):
    return x * lax.rsqrt(jnp.mean(x * x, axis=-1, keepdims=True) + RMS_EPS) * gain


def _log_sigmoid(x):
    return jnp.minimum(x, 0.0) - jnp.log1p(jnp.exp(-jnp.abs(x)))


def _row_tile(m, target):
    for t in (target, target // 2, target // 4):
        if m % t == 0:
            return t
    return m


def _ffn_body(*refs, n_chunks, final):
    if final:
        x_ref, g_ref, wup_ref, wdn_ref, gf_ref, o_ref, xn_ref, acc_ref = refs
    else:
        x_ref, g_ref, wup_ref, wdn_ref, o_ref, xn_ref, acc_ref = refs
    xn_ref[...] = _rms(x_ref[...], g_ref[...]).astype(BF16)
    acc_ref[...] = jnp.zeros_like(acc_ref)

    def chunk(c, carry):
        gu = jnp.dot(xn_ref[...], wup_ref[c], preferred_element_type=F32)
        gate, up = gu[:, :FF_CHUNK], gu[:, FF_CHUNK:]
        act = (gate * jax.nn.sigmoid(gate) * up).astype(BF16)
        acc_ref[...] += jnp.dot(act, wdn_ref[c], preferred_element_type=F32)
        return carry

    lax.fori_loop(0, n_chunks, chunk, 0, unroll=True)
    y = x_ref[...] + 0.5 * acc_ref[...]
    if final:
        y = _rms(y, gf_ref[...])
    o_ref[...] = y


def _ffn(x, gain, w_up, w_down, final_gain=None):
    m = x.shape[0]
    tm = _row_tile(m, ROW_TILE_FFN)
    n_chunks = w_up.shape[0]
    final = final_gain is not None
    row = lambda i: (i, 0)
    in_specs = [pl.BlockSpec((tm, D_MODEL), row), _resident((1, D_MODEL)),
                _resident(w_up.shape), _resident(w_down.shape)]
    args = [x, gain, w_up, w_down]
    if final:
        in_specs.append(_resident((1, D_MODEL)))
        args.append(final_gain)
    return pl.pallas_call(
        functools.partial(_ffn_body, n_chunks=n_chunks, final=final),
        grid=(m // tm,),
        in_specs=in_specs,
        out_specs=pl.BlockSpec((tm, D_MODEL), row),
        out_shape=jax.ShapeDtypeStruct((m, D_MODEL), F32),
        scratch_shapes=[pltpu.VMEM((tm, D_MODEL), BF16), pltpu.VMEM((tm, D_MODEL), F32)],
        compiler_params=_params(("parallel",)),
        name="ffn",
    )(*args)


def _inproj_body(*refs, frame):
    if frame is None:
        (x_ref, g_ref, wseg_ref, wgate_ref, qa_ref, kn_ref, ka_ref, vn_ref, va_ref, mq_ref, mk_ref, mv_ref,
         mo_ref, cb_ref, u_ref, mg_ref, ig_ref, xn_ref) = refs
    else:
        (x_ref, g_ref, wseg_ref, wgate_ref, cw_ref, qa_ref, kn_ref, ka_ref, vn_ref, va_ref, mq_ref, mk_ref, mv_ref,
         mo_ref, ycv_ref, ulast_ref, mg_ref, ig_ref, xn_ref, tail_ref) = refs
    xn_ref[...] = _rms(x_ref[...], g_ref[...]).astype(BF16)
    seg = lambda s: jnp.dot(xn_ref[...], wseg_ref[s], preferred_element_type=F32)
    qa_ref[...] = (seg(0) * QK_SCALE).astype(BF16)
    k = seg(1)
    kn_ref[...] = k.reshape(kn_ref.shape)
    ka_ref[...] = k.astype(BF16)
    v = seg(2)
    vn_ref[...] = v.reshape(vn_ref.shape)
    va_ref[...] = v.astype(BF16)
    mq_ref[...] = seg(3)
    mk_ref[...] = seg(4) * QK_SCALE
    mv_ref[...] = seg(5)
    mo_ref[...] = jax.nn.sigmoid(seg(6))
    u = seg(8) * seg(9)
    if frame is None:
        cb_ref[...] = seg(7)
        u_ref[...] = u
    else:
        tiles_per_seq, last_tile, last_row = frame
        tm = u.shape[0]
        t = pl.program_id(0) % tiles_per_seq

        @pl.when(t == 0)
        def _():
            tail_ref[...] = jnp.zeros_like(tail_ref)

        ext = jnp.concatenate([tail_ref[...], u], axis=0)
        cw = cw_ref[...]
        conv = cw[0:1] * ext[6:tm + 6] + cw[1:2] * ext[7:tm + 7] + cw[2:3] * u
        ycv_ref[...] = (seg(7) * conv).astype(BF16)
        tail_ref[...] = u[tm - 8:]

        @pl.when(t == last_tile)
        def _():
            ulast_ref[0] = u[last_row:last_row + 8]
    for s in range(6):
        mg_ref[:, s * WIDTH:(s + 1) * WIDTH] = jax.nn.sigmoid(seg(10 + s)).astype(BF16)
    ig_ref[...] = jnp.dot(xn_ref[...], wgate_ref[...], preferred_element_type=F32)


def _in_proj(x, gain, w_seg, w_gate, conv_w=None, frame=None):
    m = x.shape[0]
    tm = _row_tile(m, ROW_TILE_PROJ)
    row = lambda i: (i, 0)
    wide = lambda dt: jax.ShapeDtypeStruct((m, WIDTH), dt)
    wide_spec = pl.BlockSpec((tm, WIDTH), row)
    tail_shape = [jax.ShapeDtypeStruct((m, 3 * D_MODEL), BF16), jax.ShapeDtypeStruct((m, GATE_LANES), F32)]
    tail_specs = [pl.BlockSpec((tm, 3 * D_MODEL), row), pl.BlockSpec((tm, GATE_LANES), row)]
    in_specs = [pl.BlockSpec((tm, D_MODEL), row), _resident((1, D_MODEL)),
                _resident(w_seg.shape), _resident(w_gate.shape)]
    scratch = [pltpu.VMEM((tm, D_MODEL), BF16)]
    args = [x, gain, w_seg, w_gate]
    if frame is None:
        out_shape = [wide(BF16), wide(F32), wide(BF16), wide(F32), wide(BF16),
                     wide(F32), wide(F32), wide(F32), wide(F32), wide(F32), wide(F32)] + tail_shape
        out_specs = [wide_spec] * 11 + tail_specs
        body_frame = None
    else:
        n_seq, tf, nt = frame
        assert tf % tm == 0 and nt % 8 == 0
        tps = tf // tm
        kv_shape = jax.ShapeDtypeStruct((n_seq, nt, WIDTH), F32)
        kv_spec = pl.BlockSpec((1, tm, WIDTH), lambda i: (i // tps, i % tps, 0))
        out_shape = [wide(BF16), kv_shape, wide(BF16), kv_shape, wide(BF16), wide(F32), wide(F32), wide(F32),
                     wide(F32), wide(BF16), jax.ShapeDtypeStruct((n_seq, 8, WIDTH), F32)] + tail_shape
        out_specs = [wide_spec, kv_spec, wide_spec, kv_spec, wide_spec, wide_spec, wide_spec, wide_spec,
                     wide_spec, wide_spec, pl.BlockSpec((1, 8, WIDTH), lambda i: (i // tps, 0, 0))] + tail_specs
        in_specs.append(_resident(conv_w.shape))
        args.append(conv_w)
        scratch.append(pltpu.VMEM((8, WIDTH), F32))
        body_frame = (tps, (nt - 8) // tm, (nt - 8) % tm)
    return pl.pallas_call(
        functools.partial(_inproj_body, frame=body_frame),
        grid=(m // tm,),
        in_specs=in_specs,
        out_specs=out_specs,
        out_shape=out_shape,
        scratch_shapes=scratch,
        compiler_params=_params(("arbitrary",)),
        name="in_proj",
    )(*args)


def _strict_upper(blk):
    r = lax.broadcasted_iota(jnp.int32, (blk, blk), 0)
    c = lax.broadcasted_iota(jnp.int32, (blk, blk), 1)
    return jnp.where(r > c, 1.0, 0.0).astype(BF16)


def _sb_logs(z, valid):
    soft = jnp.log(1.0 + jnp.exp2(jnp.abs(z) * (-LOG2E)))
    log_beta = jnp.minimum(z, 0.0) - soft
    log_keep = log_beta - z
    if valid is not None:
        log_keep = jnp.where(valid, log_keep, 0.0)
    return log_beta, log_keep.astype(BF16)


def _sb_rest(log_keep, upper):
    rest = jnp.dot(log_keep, upper, preferred_element_type=F32)
    return rest, rest[:, :1] + log_keep[:, :1].astype(F32)


def _sb_local_weights(log_beta, rest, valid):
    w = jnp.exp(log_beta + rest)
    if valid is not None:
        w = jnp.where(valid, w, 0.0)
    return w.astype(BF16)


def _sb_prompt_body(bias_ref, q_ref, k_ref, v_ref, o_ref, qs_ref, uu_ref, acc_ref, carry_ref, *, blk, n_par):
    first_pair = pl.program_id(1) * n_par
    i = pl.program_id(2)
    lane = lax.broadcasted_iota(jnp.int32, (blk, PAIR), 1)
    for a in range(n_par):
        q2 = q_ref[0, :, a * PAIR:(a + 1) * PAIR]
        zero = jnp.zeros_like(q2)
        qs_ref[a, :blk] = jnp.where(lane < HEAD_DIM, q2, zero)
        qs_ref[a, blk:] = jnp.where(lane >= HEAD_DIM, q2, zero)
    uu_ref[...] = _strict_upper(blk)
    acc_ref[...] = jnp.zeros_like(acc_ref)
    carry_ref[...] = jnp.zeros_like(carry_ref)
    bias = [(bias_ref[2 * (first_pair + a)], bias_ref[2 * (first_pair + a) + 1]) for a in range(n_par)]

    def step(kb, masked):
        start = pl.multiple_of(kb * blk, blk)
        pairs = range(n_par)
        cols = lambda a: slice(a * PAIR, (a + 1) * PAIR)
        s = [lax.dot_general(qs_ref[a], k_ref[0, pl.ds(start, blk), cols(a)], _NT, preferred_element_type=F32)
             for a in pairs]
        z = [jnp.concatenate([s[a][:blk] + bias[a][0], s[a][blk:] + bias[a][1]], axis=0) for a in pairs]
        valid = None
        if masked:
            row = lax.broadcasted_iota(jnp.int32, (2 * blk, blk), 0)
            col = lax.broadcasted_iota(jnp.int32, (2 * blk, blk), 1)
            valid = col < jnp.where(row >= blk, row - blk, row)
        logs = [_sb_logs(z[a], valid) for a in pairs]
        rest = [_sb_rest(logs[a][1], uu_ref[...]) for a in pairs]
        w = [_sb_local_weights(logs[a][0], rest[a][0], valid) for a in pairs]
        for a in pairs:
            carry = carry_ref[a]
            pv = jnp.dot(w[a], v_ref[0, pl.ds(start, blk), cols(a)], preferred_element_type=F32)
            acc_ref[a] += jnp.exp(carry) * pv
            carry_ref[a] = carry + rest[a][1]

    step(i, True)

    def block(jb, carry):
        step(i - jb, False)
        return carry

    lax.fori_loop(1, i + 1, block, 0)
    for a in range(n_par):
        o_ref[0, :, a * PAIR:(a + 1) * PAIR] = jnp.where(
            lane < HEAD_DIM, acc_ref[a, :blk], acc_ref[a, blk:]).astype(BF16)


def _sb_prompt(q, k, v, bias, blk):
    bsz, tp, _ = q.shape
    nq = tp // blk
    n_par = ATT_PAIRS_PER_STEP
    wide = n_par * PAIR
    return pl.pallas_call(
        functools.partial(_sb_prompt_body, blk=blk, n_par=n_par),
        grid=(bsz, N_PAIRS // n_par, nq),
        in_specs=[pl.BlockSpec(memory_space=pltpu.SMEM),
                  pl.BlockSpec((1, blk, wide), lambda b, p, i: (b, i, p)),
                  pl.BlockSpec((1, tp, wide), lambda b, p, i: (b, 0, p)),
                  pl.BlockSpec((1, tp, wide), lambda b, p, i: (b, 0, p))],
        out_specs=pl.BlockSpec((1, blk, wide), lambda b, p, i: (b, i, p)),
        out_shape=jax.ShapeDtypeStruct((bsz, tp, WIDTH), BF16),
        scratch_shapes=[pltpu.VMEM((n_par, 2 * blk, PAIR), BF16), pltpu.VMEM((blk, blk), BF16),
                        pltpu.VMEM((n_par, 2 * blk, PAIR), F32), pltpu.VMEM((n_par, 2 * blk, 1), F32)],
        compiler_params=_params(("parallel", "parallel", "arbitrary")),
        name="sb_prompt",
    )(bias, q, k, v)


def _sb_paged_body(pt_ref, bias_ref, q_ref, kn_ref, vn_ref, *rest, tq, group, n_steps):
    del pt_ref
    k_refs = rest[:group]
    v_refs = rest[group:2 * group]
    o_ref, qbd_ref, bias_t_ref, uu_ref, acc_ref, carry_ref = rest[2 * group:]
    j = pl.program_id(1)
    rows = HEADS * tq

    def sweep(kts, vts, carry, acc, valid):
        n = len(kts)
        kt = jnp.concatenate([t.astype(BF16) for t in kts], axis=1)
        vt = jnp.concatenate([t.astype(BF16) for t in vts], axis=1)
        s = jnp.dot(qbd_ref[...], kt, preferred_element_type=F32)
        page = lambda a, g: a[:, g * PAGE_SIZE:(g + 1) * PAGE_SIZE]
        logs = [_sb_logs(page(s, g) + bias_t_ref[...], valid) for g in range(n)]
        rest, total = _sb_rest(jnp.concatenate([lg[1] for lg in logs], axis=0), uu_ref[...])
        w = []
        for g in range(n):
            rows_g = slice(g * rows, (g + 1) * rows)
            w.append(_sb_local_weights(logs[g][0] + carry, rest[rows_g], valid))
            carry = carry + total[rows_g]
        w = jnp.concatenate(w, axis=1)
        return carry, acc + lax.dot_general(w, vt, _NT, preferred_element_type=F32)

    @pl.when(j == 0)
    def _():
        rw = lax.broadcasted_iota(jnp.int32, (rows, WIDTH), 0)
        cw = lax.broadcasted_iota(jnp.int32, (rows, WIDTH), 1)
        q_all = jnp.concatenate([q_ref[0]] * HEADS, axis=0)
        qbd_ref[...] = jnp.where(rw // tq == cw // HEAD_DIM, q_all, 0.0).astype(BF16)
        row = lax.broadcasted_iota(jnp.int32, (rows, PAGE_SIZE), 0)
        col = lax.broadcasted_iota(jnp.int32, (rows, PAGE_SIZE), 1)
        bias_t = jnp.zeros((rows, PAGE_SIZE), F32)
        for h in range(HEADS):
            bias_t = jnp.where(row // tq == h, bias_ref[h], bias_t)
        bias_t_ref[...] = bias_t
        uu_ref[...] = _strict_upper(PAGE_SIZE)
        valid = col < row % tq
        carry, acc = sweep([kn_ref[0]], [vn_ref[0]], jnp.zeros((rows, 1), F32), jnp.zeros((rows, WIDTH), F32), valid)
        carry_ref[...] = carry
        acc_ref[...] = acc

    carry, acc = sweep([r[0, 0] for r in k_refs], [r[0, 0] for r in v_refs], carry_ref[...], acc_ref[...], None)
    carry_ref[...] = carry
    acc_ref[...] = acc

    @pl.when(j == n_steps - 1)
    def _():
        rw = lax.broadcasted_iota(jnp.int32, (tq, WIDTH), 1)
        out = jnp.zeros((tq, WIDTH), F32)
        for h in range(HEADS):
            out = jnp.where(rw // HEAD_DIM == h, acc_ref[h * tq:(h + 1) * tq], out)
        o_ref[0] = out


def _sb_paged(q, kt_new, vt_new, cache_kt, cache_vt, page_table, bias, layer):
    bsz, tq, _ = q.shape
    n_pages = page_table.shape[1]
    group = next(g for g in (PAGE_GROUP, 4, 2, 1) if n_pages % g == 0)
    n_steps = n_pages // group
    rows = HEADS * tq
    per_b = lambda b, j, pt: (b, 0, 0)

    def page(g):
        return lambda b, j, pt: (layer, pt[b, n_pages - 1 - (j * group + g)], 0, 0)

    page_specs = [pl.BlockSpec((1, 1, WIDTH, PAGE_SIZE), page(g)) for g in range(group)]
    grid_spec = pltpu.PrefetchScalarGridSpec(
        num_scalar_prefetch=1,
        grid=(bsz, n_steps),
        in_specs=[pl.BlockSpec(memory_space=pltpu.SMEM),
                  pl.BlockSpec((1, tq, WIDTH), per_b),
                  pl.BlockSpec((1, WIDTH, PAGE_SIZE), per_b),
                  pl.BlockSpec((1, WIDTH, PAGE_SIZE), per_b)] + page_specs + page_specs,
        out_specs=pl.BlockSpec((1, tq, WIDTH), per_b),
        scratch_shapes=[pltpu.VMEM((rows, WIDTH), BF16), pltpu.VMEM((rows, PAGE_SIZE), F32),
                        pltpu.VMEM((PAGE_SIZE, PAGE_SIZE), BF16),
                        pltpu.VMEM((rows, WIDTH), F32), pltpu.VMEM((rows, 1), F32)])
    return pl.pallas_call(
        functools.partial(_sb_paged_body, tq=tq, group=group, n_steps=n_steps),
        grid_spec=grid_spec,
        out_shape=jax.ShapeDtypeStruct((bsz, tq, WIDTH), F32),
        compiler_params=_params(("parallel", "arbitrary")),
        name="sb_paged",
    )(page_table, bias, q, kt_new, vt_new, *([cache_kt] * group), *([cache_vt] * group))


def _mlstm_body(q_ref, k_ref, v_ref, og_ref, gc_ref, bc_ref, hg_ref, c0_ref, n0_ref, m0_ref,
                y_ref, c_out_ref, n_out_ref, m_out_ref, c_s, n_s, m_s, *, chunk, n_seq, n_tok, n_real, n_chunks):
    ci = pl.program_id(1)

    @pl.when(ci == 0)
    def _():
        c_s[...] = c0_ref[...]
        n_s[...] = n0_ref[...]
        m_s[...] = m0_ref[...]

    L = chunk
    short = n_tok < L

    def tokens(ref, b, cols=slice(None)):
        x = ref[b, :, cols]
        if short:
            x = jnp.concatenate([x, jnp.zeros((L - n_tok, x.shape[1]), x.dtype)], axis=0)
        return x

    r_i = lax.broadcasted_iota(jnp.int32, (L, L), 0)
    c_i = lax.broadcasted_iota(jnp.int32, (L, L), 1)
    causal = r_i >= c_i
    tri = jnp.where(causal, 1.0, 0.0).astype(F32)
    lane_g = lax.broadcasted_iota(jnp.int32, (L, GATE_LANES), 1)
    lane = lax.broadcasted_iota(jnp.int32, (L, PAIR), 1)
    lane1 = lax.broadcasted_iota(jnp.int32, (1, PAIR), 1)
    blk_r = lax.broadcasted_iota(jnp.int32, (PAIR, PAIR), 0)
    blk_c = lax.broadcasted_iota(jnp.int32, (PAIR, PAIR), 1)
    same_head = (blk_r < HEAD_DIM) == (blk_c < HEAD_DIM)
    lane_m = lax.broadcasted_iota(jnp.int32, (1, HEADS), 1)
    mine = [lane < HEAD_DIM, lane >= HEAD_DIM]
    cols = [slice(p * PAIR, (p + 1) * PAIR) for p in range(N_PAIRS)]
    seqs = range(n_seq)
    probs = [(b, h) for b in seqs for h in range(HEADS)]
    pair_ids = [(b, p) for b in seqs for p in range(N_PAIRS)]

    x_c, cum_c, x_r, cum_r = {}, {}, {}, {}
    for b in seqs:
        gate_c = tokens(gc_ref, b) + bc_ref[...]
        xc = jnp.where(lane_g >= HEADS, _log_sigmoid(gate_c), gate_c)
        if n_real < n_chunks * L:
            tok_c = ci * L + lax.broadcasted_iota(jnp.int32, (L, GATE_LANES), 0)
            xc = jnp.where(tok_c < n_real, xc, jnp.where(lane_g >= HEADS, 0.0, NEG_BIG))
        cc = jnp.dot(tri, xc, preferred_element_type=F32, precision=lax.Precision.HIGHEST)
        x_c[b], cum_c[b] = xc, cc
        x_r[b] = xc.T[:2 * HEADS]
        cum_r[b] = cc.T[:2 * HEADS]

    q2 = {bp: tokens(q_ref, bp[0], cols[bp[1]]) for bp in pair_ids}
    k2 = {bp: tokens(k_ref, bp[0], cols[bp[1]]) for bp in pair_ids}
    k2b = {bp: k2[bp].astype(BF16) for bp in pair_ids}
    v2b = {bp: tokens(v_ref, bp[0], cols[bp[1]]).astype(BF16) for bp in pair_ids}
    c_pair = {bp: c_s[bp[0], bp[1]] for bp in pair_ids}
    n_row = {bp: n_s[bp[0], bp[1]] for bp in pair_ids}
    pair_of = lambda bh: (bh[0], bh[1] // 2)
    qm = {bh: jnp.where(mine[bh[1] % 2], q2[pair_of(bh)], 0.0) for bh in probs}
    qmb = {bh: qm[bh].astype(BF16) for bh in probs}

    b_c, i_c, b_r, i_r, m_prev, m_new, wend, decay = {}, {}, {}, {}, {}, {}, {}, {}
    for bh in probs:
        b, h = bh
        b_c[bh] = cum_c[b][:, HEADS + h:HEADS + h + 1]
        i_c[bh] = x_c[b][:, h:h + 1]
        b_r[bh] = cum_r[b][HEADS + h:HEADS + h + 1, :]
        i_r[bh] = x_r[b][h:h + 1, :]
        m_prev[bh] = m_s[b][:, h:h + 1]
        b_last = b_r[bh][:, L - 1:L]
        m_new[bh] = jnp.maximum(b_last + m_prev[bh],
                                jnp.max(b_last - b_r[bh] + i_r[bh], axis=1, keepdims=True))
        wend[bh] = jnp.exp(b_last - b_c[bh] + i_c[bh] - m_new[bh])
        decay[bh] = jnp.exp(b_last + m_prev[bh] - m_new[bh])
    kw = {(b, p): k2[(b, p)] * jnp.where(mine[1], wend[(b, 2 * p + 1)], wend[(b, 2 * p)]) for b, p in pair_ids}

    qk = {bh: lax.dot_general(qmb[bh], k2b[pair_of(bh)], _NT, preferred_element_type=F32) for bh in probs}
    qc = {bh: jnp.dot(qmb[bh], c_pair[pair_of(bh)].astype(BF16), preferred_element_type=F32) for bh in probs}
    upd = {bp: jnp.dot(kw[bp].T.astype(BF16), v2b[bp], preferred_element_type=F32) for bp in pair_ids}

    s, w_inter, m_t = {}, {}, {}
    for bh in probs:
        log_d = jnp.where(causal, b_c[bh] - b_r[bh] + i_r[bh], NEG_BIG)
        inter = b_c[bh] + m_prev[bh]
        m_t[bh] = jnp.maximum(inter, jnp.max(log_d, axis=1, keepdims=True))
        s[bh] = qk[bh] * jnp.exp(log_d - m_t[bh])
        w_inter[bh] = jnp.exp(inter - m_t[bh])
    sv = {bh: jnp.dot(s[bh].astype(BF16), v2b[pair_of(bh)], preferred_element_type=F32) for bh in probs}

    hid = {}
    for bh in probs:
        num = sv[bh] + w_inter[bh] * qc[bh]
        den = (jnp.sum(s[bh], axis=1, keepdims=True)
               + w_inter[bh] * jnp.sum(qm[bh] * n_row[pair_of(bh)], axis=1, keepdims=True))
        x = num * (1.0 / jnp.maximum(jnp.abs(den), jnp.exp(-m_t[bh])))
        ms = jnp.sum(jnp.where(mine[bh[1] % 2], x * x, 0.0), axis=1, keepdims=True) * (1.0 / HEAD_DIM)
        hid[bh] = x * lax.rsqrt(ms + RMS_EPS)

    for b in seqs:
        m_next = m_s[b]
        for h in range(HEADS):
            m_next = jnp.where(lane_m == h, m_new[(b, h)], m_next)
        m_s[b] = m_next
    for b, p in pair_ids:
        h_pair = jnp.where(mine[1], hid[(b, 2 * p + 1)], hid[(b, 2 * p)])
        y = tokens(og_ref, b, cols[p]) * h_pair * hg_ref[:, cols[p]]
        y_ref[b, :, cols[p]] = y[:n_tok].astype(y_ref.dtype) if short else y.astype(y_ref.dtype)
        decay_pair = jnp.where(lane1 >= HEAD_DIM, decay[(b, 2 * p + 1)], decay[(b, 2 * p)])
        c_s[b, p] = decay_pair * c_pair[(b, p)] + jnp.where(same_head, upd[(b, p)], 0.0)
        n_s[b, p] = decay_pair * n_row[(b, p)] + jnp.sum(kw[(b, p)], axis=0, keepdims=True)

    @pl.when(ci == n_chunks - 1)
    def _():
        c_out_ref[...] = c_s[...]
        n_out_ref[...] = n_s[...]
        m_out_ref[...] = m_s[...]


def _pack_pairs(c):
    bsz = c.shape[0]
    c = c.reshape(bsz, N_PAIRS, 2, HEAD_DIM, HEAD_DIM)
    z = jnp.zeros_like(c[:, :, 0])
    top = jnp.concatenate([c[:, :, 0], z], axis=-1)
    bot = jnp.concatenate([z, c[:, :, 1]], axis=-1)
    return jnp.concatenate([top, bot], axis=-2)


def _unpack_pairs(cp):
    bsz = cp.shape[0]
    a = cp[:, :, :HEAD_DIM, :HEAD_DIM]
    b = cp[:, :, HEAD_DIM:, HEAD_DIM:]
    return jnp.stack([a, b], axis=2).reshape(bsz, HEADS, HEAD_DIM, HEAD_DIM)


def _mlstm(q, k, v, og, gates, bias_c, head_gain, c0, n0, m0, n_real):
    bsz, t, _ = q.shape
    L = ML_CHUNK
    n_chunks = max(t // L, 1)
    tb = min(t, L)
    assert t == n_chunks * tb
    nb = ML_SEQS_PER_STEP if bsz % ML_SEQS_PER_STEP == 0 else 1
    y_dtype = BF16 if tb == L else F32
    tok = lambda b, c: (b, c, 0)
    per_b3 = lambda b, c: (b, 0, 0)
    per_b4 = lambda b, c: (b, 0, 0, 0)
    y, c_out, n_out, m_out = pl.pallas_call(
        functools.partial(_mlstm_body, chunk=L, n_seq=nb, n_tok=tb, n_real=n_real, n_chunks=n_chunks),
        grid=(bsz // nb, n_chunks),
        in_specs=[pl.BlockSpec((nb, tb, WIDTH), tok)] * 4 + [
            pl.BlockSpec((nb, tb, GATE_LANES), tok),
            pl.BlockSpec((1, GATE_LANES), lambda b, c: (0, 0)),
            pl.BlockSpec((1, WIDTH), lambda b, c: (0, 0)),
            pl.BlockSpec((nb, N_PAIRS, PAIR, PAIR), per_b4),
            pl.BlockSpec((nb, N_PAIRS, 1, PAIR), per_b4),
            pl.BlockSpec((nb, 1, HEADS), per_b3)],
        out_specs=[pl.BlockSpec((nb, tb, WIDTH), tok),
                   pl.BlockSpec((nb, N_PAIRS, PAIR, PAIR), per_b4),
                   pl.BlockSpec((nb, N_PAIRS, 1, PAIR), per_b4),
                   pl.BlockSpec((nb, 1, HEADS), per_b3)],
        out_shape=[jax.ShapeDtypeStruct((bsz, t, WIDTH), y_dtype),
                   jax.ShapeDtypeStruct((bsz, N_PAIRS, PAIR, PAIR), F32),
                   jax.ShapeDtypeStruct((bsz, N_PAIRS, 1, PAIR), F32),
                   jax.ShapeDtypeStruct((bsz, 1, HEADS), F32)],
        scratch_shapes=[pltpu.VMEM((nb, N_PAIRS, PAIR, PAIR), F32), pltpu.VMEM((nb, N_PAIRS, 1, PAIR), F32),
                        pltpu.VMEM((nb, 1, HEADS), F32)],
        compiler_params=_params(("parallel", "arbitrary")),
        name="mlstm",
    )(q, k, v, og, gates, bias_c, head_gain,
      _pack_pairs(c0), n0.reshape(bsz, N_PAIRS, 1, PAIR), m0.reshape(bsz, 1, HEADS))
    return (y, _unpack_pairs(c_out), n_out.reshape(bsz, HEADS, HEAD_DIM), m_out.reshape(bsz, HEADS))


def _merge_body(*refs, conv_here):
    if conv_here:
        x_ref, ysb_ref, yml_ref, cb_ref, u_ref, u1_ref, u2_ref, mg_ref, cw_ref, wbr_ref, wout_ref, o_ref = refs
        cw = cw_ref[...]
        conv = cw[0:1] * u2_ref[...] + cw[1:2] * u1_ref[...] + cw[2:3] * u_ref[...]
        ycv = (cb_ref[...] * conv).astype(BF16)
    else:
        x_ref, ysb_ref, yml_ref, ycv_ref, mg_ref, wbr_ref, wout_ref, o_ref = refs
        ycv = ycv_ref[...]
    gate = lambda j: mg_ref[:, j * D_MODEL:(j + 1) * D_MODEL].astype(F32)
    merged = (gate(0) * jnp.dot(ysb_ref[...], wbr_ref[0], preferred_element_type=F32)
              + gate(1) * jnp.dot(yml_ref[...], wbr_ref[1], preferred_element_type=F32)
              + gate(2) * jnp.dot(ycv, wbr_ref[2], preferred_element_type=F32))
    o_ref[...] = x_ref[...] + jnp.dot(merged.astype(BF16), wout_ref[...], preferred_element_type=F32)


def _merge(x, ysb, yml, conv_inputs, mg, w_br, w_out):
    m = x.shape[0]
    tm = _row_tile(m, ROW_TILE_PROJ)
    row = lambda i: (i, 0)
    wide = pl.BlockSpec((tm, WIDTH), row)
    conv_here = len(conv_inputs) > 1
    conv_specs = [wide] * 4 if conv_here else [wide]
    weights = ([conv_inputs[4]] if conv_here else []) + [w_br, w_out]
    return pl.pallas_call(
        functools.partial(_merge_body, conv_here=conv_here),
        grid=(m // tm,),
        in_specs=[pl.BlockSpec((tm, D_MODEL), row), wide, wide] + conv_specs + [
            pl.BlockSpec((tm, 3 * D_MODEL), row)] + [_resident(w.shape) for w in weights],
        out_specs=pl.BlockSpec((tm, D_MODEL), row),
        out_shape=jax.ShapeDtypeStruct((m, D_MODEL), F32),
        compiler_params=_params(("parallel",)),
        name="merge",
    )(x, ysb, yml, *conv_inputs[:4], mg, *weights)


def _shifted(u, buf):
    ext = jnp.concatenate([buf, u], axis=1)
    t = u.shape[1]
    return ext[:, 1:t + 1], ext[:, 0:t]


def kernel(x_prompt, x_sample, cache_sb_k, cache_sb_v, page_table, state_ml_C, state_ml_n, state_ml_m,
           state_conv, meta_tokens, norm_ff1, ffn1_w_up, ffn1_w_down, norm_mix, w_in, sb_logit_bias,
           ml_igate_bias, ml_fgate_bias, ml_head_norm, conv_w, w_branch_sb, w_branch_ml, w_branch_cv, w_out,
           norm_ff2, ffn2_w_up, ffn2_w_down, norm_final):
    bp, seq, _ = x_prompt.shape
    bs, ts, _ = x_sample.shape
    depth = w_in.shape[0]
    d_ff = ffn1_w_down.shape[1]
    n_ff = d_ff // FF_CHUNK
    n_small = bs * ts
    blk = ATT_BLOCK
    nt = N_META + seq
    tf = -(-nt // blk) * blk
    assert blk % ML_CHUNK == 0 and d_ff % FF_CHUNK == 0
    assert ts % 8 == 0 and ts <= ML_CHUNK

    def ffn_weights(w_up, w_down):
        gate = w_up[:, :d_ff].reshape(D_MODEL, n_ff, FF_CHUNK)
        up = w_up[:, d_ff:].reshape(D_MODEL, n_ff, FF_CHUNK)
        wu = jnp.concatenate([gate, up], axis=-1).transpose(1, 0, 2).astype(BF16)
        return wu, w_down.reshape(n_ff, FF_CHUNK, D_MODEL).astype(BF16)

    def in_weights(w):
        main = jnp.concatenate([w[:, :6 * WIDTH], w[:, 6 * WIDTH + 2 * HEADS:]], axis=1)
        w_seg = main.reshape(D_MODEL, N_SEG, WIDTH).transpose(1, 0, 2).astype(BF16)
        gate = w[:, 6 * WIDTH:6 * WIDTH + 2 * HEADS]
        w_gate = jnp.pad(gate, ((0, 0), (0, GATE_LANES - 2 * HEADS))).astype(BF16)
        return w_seg, w_gate

    row1 = lambda a: a.reshape(1, -1)
    meta = jnp.broadcast_to(meta_tokens.astype(F32)[None], (bp, N_META, D_MODEL))
    xb = jnp.concatenate([meta, x_prompt, jnp.zeros((bp, tf - nt, D_MODEL), F32)], axis=1).reshape(bp * tf, D_MODEL)
    xs = x_sample.reshape(n_small, D_MODEL)
    pages_t = lambda c: jnp.transpose(c, (0, 1, 3, 4, 2)).reshape(c.shape[:2] + (WIDTH, PAGE_SIZE))
    cache_kt = pages_t(cache_sb_k)
    cache_vt = pages_t(cache_sb_v)
    frame = lambda a: a.reshape(bp, tf, a.shape[-1])
    smp = lambda a: a.reshape(bs, ts, a.shape[-1])
    flat = lambda a: a.reshape(-1, a.shape[-1])
    heads = lambda a: a.reshape(a.shape[:-1] + (HEADS, HEAD_DIM))

    new_p, new_s = [], []
    for l in range(depth):
        last = l == depth - 1
        wu1, wd1 = ffn_weights(ffn1_w_up[l], ffn1_w_down[l])
        wu2, wd2 = ffn_weights(ffn2_w_up[l], ffn2_w_down[l])
        w_seg, w_gate = in_weights(w_in[l])
        w_br = jnp.stack([w_branch_sb[l], w_branch_ml[l], w_branch_cv[l]]).astype(BF16)
        w_o = w_out[l].astype(BF16)
        bias_c = jnp.pad(jnp.concatenate([ml_igate_bias[l], ml_fgate_bias[l]]),
                         (0, GATE_LANES - 2 * HEADS)).reshape(1, GATE_LANES)
        ml_args = (bias_c, row1(ml_head_norm[l]))

        xb = _ffn(xb, row1(norm_ff1[l]), wu1, wd1)
        xs = _ffn(xs, row1(norm_ff1[l]), wu1, wd1)
        (qa_b, kn_b, ka_b, vn_b, va_b, mq_b, mk_b, mv_b, og_b, ycv_b, ulast_b, mg_b, ig_b) = _in_proj(
            xb, row1(norm_mix[l]), w_seg, w_gate, conv_w[l], (bp, tf, nt))
        (qa_s, kn_s, ka_s, vn_s, va_s, mq_s, mk_s, mv_s, og_s, cb_s, u_s, mg_s, ig_s) = _in_proj(
            xs, row1(norm_mix[l]), w_seg, w_gate)

        ysb_b = flat(_sb_prompt(frame(qa_b), frame(ka_b), frame(va_b), sb_logit_bias[l], blk))
        new_t = lambda a: jnp.pad(jnp.swapaxes(smp(a), 1, 2), ((0, 0), (0, 0), (0, PAGE_SIZE - ts)))
        ysb_s = flat(_sb_paged(smp(qa_s).astype(F32), new_t(kn_s), new_t(vn_s), cache_kt, cache_vt, page_table,
                               sb_logit_bias[l], l)).astype(BF16)

        yml_b, c_p, n_p, m_p = _mlstm(
            frame(mq_b), frame(mk_b), frame(mv_b), frame(og_b), frame(ig_b), *ml_args,
            jnp.zeros((bp, HEADS, HEAD_DIM, HEAD_DIM), F32), jnp.zeros((bp, HEADS, HEAD_DIM), F32),
            jnp.zeros((bp, HEADS), F32), nt)
        yml_s, c_s, n_s, m_s = _mlstm(
            smp(mq_s), smp(mk_s), smp(mv_s), smp(og_s), smp(ig_s), *ml_args,
            state_ml_C[l].astype(F32), state_ml_n[l].astype(F32), state_ml_m[l].astype(F32), ts)

        u_smp = smp(u_s)
        u1_s, u2_s = _shifted(u_smp, state_conv[l].astype(F32))
        xb = _merge(xb, ysb_b, flat(yml_b), (ycv_b,), mg_b, w_br, w_o)
        xs = _merge(xs, ysb_s, flat(yml_s).astype(BF16), (cb_s, u_s, flat(u1_s), flat(u2_s), conv_w[l]), mg_s,
                    w_br, w_o)

        fin = row1(norm_final) if last else None
        xb = _ffn(xb, row1(norm_ff2[l]), wu2, wd2, fin)
        xs = _ffn(xs, row1(norm_ff2[l]), wu2, wd2, fin)

        new_p.append((heads(kn_b), heads(vn_b), c_p, n_p, m_p, ulast_b[:, 6:]))
        new_s.append((heads(smp(kn_s)), heads(smp(vn_s)), c_s, n_s, m_s,
                      jnp.concatenate([state_conv[l].astype(F32), u_smp], axis=1)[:, -2:]))

    y_prompt = frame(xb)[:, N_META:nt]
    y_sample = xs.reshape(bs, ts, D_MODEL)
    stk = lambda states, i: jnp.stack([st[i] for st in states])
    return (y_prompt, y_sample) + tuple(stk(new_p, i) for i in range(6)) + tuple(stk(new_s, i) for i in range(6))
```

```python
import functools

import jax
import jax.numpy as jnp
from jax import lax
from jax.experimental import pallas as pl
from jax.experimental.pallas import tpu as pltpu

F32 = jnp.float32
BF16 = jnp.bfloat16

D_MODEL = 1024
N_META = 16
HEADS = 8
HEAD_DIM = 64
WIDTH = HEADS * HEAD_DIM
PAIR = 2 * HEAD_DIM
N_PAIRS = HEADS // 2
PAGE_SIZE = 128
RMS_EPS = 1e-6
QK_SCALE = HEAD_DIM ** -0.5
NEG_BIG = -1e30
LOG2E = 1.4426950408889634

FF_CHUNK = 256
ROW_TILE_FFN = 512
ROW_TILE_PROJ = 256
ATT_BLOCK = 256
ATT_PAIRS_PER_STEP = 4
ML_CHUNK = 128
ML_SEQS_PER_STEP = 2
PAGE_GROUP = 16
N_SEG = 16
GATE_LANES = 128
VMEM_LIMIT = 56 * 1024 * 1024

_NT = (((1,), (1,)), ((), ()))


def _params(semantics):
    return pltpu.CompilerParams(dimension_semantics=semantics, vmem_limit_bytes=VMEM_LIMIT)


def _resident(shape):
    nd = len(shape)
    return pl.BlockSpec(shape, lambda *_: (0,) * nd, pipeline_mode=pl.Buffered(1))


def _rms(x, gain):
    return x * lax.rsqrt(jnp.mean(x * x, axis=-1, keepdims=True) + RMS_EPS) * gain


def _log_sigmoid(x):
    return jnp.minimum(x, 0.0) - jnp.log1p(jnp.exp(-jnp.abs(x)))


def _row_tile(m, target):
    for t in (target, target // 2, target // 4):
        if m % t == 0:
            return t
    return m


def _ffn_body(*refs, n_chunks, final):
    if final:
        x_ref, g_ref, wup_ref, wdn_ref, gf_ref, o_ref, xn_ref, acc_ref = refs
    else:
        x_ref, g_ref, wup_ref, wdn_ref, o_ref, xn_ref, acc_ref = refs
    xn_ref[...] = _rms(x_ref[...], g_ref[...]).astype(BF16)
    acc_ref[...] = jnp.zeros_like(acc_ref)

    def chunk(c, carry):
        gu = jnp.dot(xn_ref[...], wup_ref[c], preferred_element_type=F32)
        gate, up = gu[:, :FF_CHUNK], gu[:, FF_CHUNK:]
        act = (gate * jax.nn.sigmoid(gate) * up).astype(BF16)
        acc_ref[...] += jnp.dot(act, wdn_ref[c], preferred_element_type=F32)
        return carry

    lax.fori_loop(0, n_chunks, chunk, 0, unroll=True)
    y = x_ref[...] + 0.5 * acc_ref[...]
    if final:
        y = _rms(y, gf_ref[...])
    o_ref[...] = y


def _ffn(x, gain, w_up, w_down, final_gain=None):
    m = x.shape[0]
    tm = _row_tile(m, ROW_TILE_FFN)
    n_chunks = w_up.shape[0]
    final = final_gain is not None
    row = lambda i: (i, 0)
    in_specs = [pl.BlockSpec((tm, D_MODEL), row), _resident((1, D_MODEL)),
                _resident(w_up.shape), _resident(w_down.shape)]
    args = [x, gain, w_up, w_down]
    if final:
        in_specs.append(_resident((1, D_MODEL)))
        args.append(final_gain)
    return pl.pallas_call(
        functools.partial(_ffn_body, n_chunks=n_chunks, final=final),
        grid=(m // tm,),
        in_specs=in_specs,
        out_specs=pl.BlockSpec((tm, D_MODEL), row),
        out_shape=jax.ShapeDtypeStruct((m, D_MODEL), F32),
        scratch_shapes=[pltpu.VMEM((tm, D_MODEL), BF16), pltpu.VMEM((tm, D_MODEL), F32)],
        compiler_params=_params(("parallel",)),
        name="ffn",
    )(*args)


def _inproj_body(*refs, frame):
    if frame is None:
        (x_ref, g_ref, wseg_ref, wgate_ref, qa_ref, kn_ref, ka_ref, vn_ref, va_ref, mq_ref, mk_ref, mv_ref,
         mo_ref, cb_ref, u_ref, mg_ref, ig_ref, xn_ref) = refs
    else:
        (x_ref, g_ref, wseg_ref, wgate_ref, cw_ref, qa_ref, kn_ref, ka_ref, vn_ref, va_ref, mq_ref, mk_ref, mv_ref,
         mo_ref, ycv_ref, ulast_ref, mg_ref, ig_ref, xn_ref, tail_ref) = refs
    xn_ref[...] = _rms(x_ref[...], g_ref[...]).astype(BF16)
    seg = lambda s: jnp.dot(xn_ref[...], wseg_ref[s], preferred_element_type=F32)
    qa_ref[...] = (seg(0) * QK_SCALE).astype(BF16)
    k = seg(1)
    kn_ref[...] = k.reshape(kn_ref.shape)
    ka_ref[...] = k.astype(BF16)
    v = seg(2)
    vn_ref[...] = v.reshape(vn_ref.shape)
    va_ref[...] = v.astype(BF16)
    mq_ref[...] = seg(3)
    mk_ref[...] = seg(4) * QK_SCALE
    mv_ref[...] = seg(5)
    mo_ref[...] = jax.nn.sigmoid(seg(6))
    u = seg(8) * seg(9)
    if frame is None:
        cb_ref[...] = seg(7)
        u_ref[...] = u
    else:
        tiles_per_seq, last_tile, last_row = frame
        tm = u.shape[0]
        t = pl.program_id(0) % tiles_per_seq

        @pl.when(t == 0)
        def _():
            tail_ref[...] = jnp.zeros_like(tail_ref)

        ext = jnp.concatenate([tail_ref[...], u], axis=0)
        cw = cw_ref[...]
        conv = cw[0:1] * ext[6:tm + 6] + cw[1:2] * ext[7:tm + 7] + cw[2:3] * u
        ycv_ref[...] = (seg(7) * conv).astype(BF16)
        tail_ref[...] = u[tm - 8:]

        @pl.when(t == last_tile)
        def _():
            ulast_ref[0] = u[last_row:last_row + 8]
    for s in range(6):
        mg_ref[:, s * WIDTH:(s + 1) * WIDTH] = jax.nn.sigmoid(seg(10 + s)).astype(BF16)
    ig_ref[...] = jnp.dot(xn_ref[...], wgate_ref[...], preferred_element_type=F32)


def _in_proj(x, gain, w_seg, w_gate, conv_w=None, frame=None):
    m = x.shape[0]
    tm = _row_tile(m, ROW_TILE_PROJ)
    row = lambda i: (i, 0)
    wide = lambda dt: jax.ShapeDtypeStruct((m, WIDTH), dt)
    wide_spec = pl.BlockSpec((tm, WIDTH), row)
    tail_shape = [jax.ShapeDtypeStruct((m, 3 * D_MODEL), BF16), jax.ShapeDtypeStruct((m, GATE_LANES), F32)]
    tail_specs = [pl.BlockSpec((tm, 3 * D_MODEL), row), pl.BlockSpec((tm, GATE_LANES), row)]
    in_specs = [pl.BlockSpec((tm, D_MODEL), row), _resident((1, D_MODEL)),
                _resident(w_seg.shape), _resident(w_gate.shape)]
    scratch = [pltpu.VMEM((tm, D_MODEL), BF16)]
    args = [x, gain, w_seg, w_gate]
    if frame is None:
        out_shape = [wide(BF16), wide(F32), wide(BF16), wide(F32), wide(BF16),
                     wide(F32), wide(F32), wide(F32), wide(F32), wide(F32), wide(F32)] + tail_shape
        out_specs = [wide_spec] * 11 + tail_specs
        body_frame = None
    else:
        n_seq, tf, nt = frame
        assert tf % tm == 0 and nt % 8 == 0
        tps = tf // tm
        kv_shape = jax.ShapeDtypeStruct((n_seq, nt, WIDTH), F32)
        kv_spec = pl.BlockSpec((1, tm, WIDTH), lambda i: (i // tps, i % tps, 0))
        out_shape = [wide(BF16), kv_shape, wide(BF16), kv_shape, wide(BF16), wide(F32), wide(F32), wide(F32),
                     wide(F32), wide(BF16), jax.ShapeDtypeStruct((n_seq, 8, WIDTH), F32)] + tail_shape
        out_specs = [wide_spec, kv_spec, wide_spec, kv_spec, wide_spec, wide_spec, wide_spec, wide_spec,
                     wide_spec, wide_spec, pl.BlockSpec((1, 8, WIDTH), lambda i: (i // tps, 0, 0))] + tail_specs
        in_specs.append(_resident(conv_w.shape))
        args.append(conv_w)
        scratch.append(pltpu.VMEM((8, WIDTH), F32))
        body_frame = (tps, (nt - 8) // tm, (nt - 8) % tm)
    return pl.pallas_call(
        functools.partial(_inproj_body, frame=body_frame),
        grid=(m // tm,),
        in_specs=in_specs,
        out_specs=out_specs,
        out_shape=out_shape,
        scratch_shapes=scratch,
        compiler_params=_params(("arbitrary",)),
        name="in_proj",
    )(*args)


def _strict_upper(blk):
    r = lax.broadcasted_iota(jnp.int32, (blk, blk), 0)
    c = lax.broadcasted_iota(jnp.int32, (blk, blk), 1)
    return jnp.where(r > c, 1.0, 0.0).astype(BF16)


def _sb_logs(z, valid):
    soft = jnp.log(1.0 + jnp.exp2(jnp.abs(z) * (-LOG2E)))
    log_beta = jnp.minimum(z, 0.0) - soft
    log_keep = log_beta - z
    if valid is not None:
        log_keep = jnp.where(valid, log_keep, 0.0)
    return log_beta, log_keep.astype(BF16)


def _sb_rest(log_keep, upper):
    rest = jnp.dot(log_keep, upper, preferred_element_type=F32)
    return rest, rest[:, :1] + log_keep[:, :1].astype(F32)


def _sb_local_weights(log_beta, rest, valid):
    w = jnp.exp(log_beta + rest)
    if valid is not None:
        w = jnp.where(valid, w, 0.0)
    return w.astype(BF16)


def _sb_prompt_body(bias_ref, q_ref, k_ref, vt_ref, o_ref, qt_ref, mt_ref, acc_ref, carry_ref, *, blk, n_par):
    first_pair = pl.program_id(1) * n_par
    i = pl.program_id(2)
    pairs = range(n_par)
    cols = [slice(a * PAIR, (a + 1) * PAIR) for a in pairs]
    dim = lax.broadcasted_iota(jnp.int32, (PAIR, blk), 0)
    for a in pairs:
        qt = q_ref[0, :, cols[a]].astype(F32).T
        qt_ref[a, :, :blk] = jnp.where(dim < HEAD_DIM, qt, 0.0).astype(BF16)
        qt_ref[a, :, blk:] = jnp.where(dim >= HEAD_DIM, qt, 0.0).astype(BF16)
    r = lax.broadcasted_iota(jnp.int32, (blk + 8, blk), 0)
    c = lax.broadcasted_iota(jnp.int32, (blk + 8, blk), 1)
    mt_ref[...] = jnp.where(jnp.logical_or(c > r, r >= blk), 1.0, 0.0).astype(BF16)
    acc_ref[...] = jnp.zeros_like(acc_ref)
    carry_ref[...] = jnp.zeros_like(carry_ref)
    second = lax.broadcasted_iota(jnp.int32, (1, 2 * blk), 1) >= blk
    bias = [jnp.where(second, bias_ref[2 * (first_pair + a) + 1], bias_ref[2 * (first_pair + a)]) for a in pairs]

    def step(kb, masked):
        start = pl.multiple_of(kb * blk, blk)
        z = [jnp.dot(k_ref[0, pl.ds(start, blk), cols[a]], qt_ref[a], preferred_element_type=F32) + bias[a]
             for a in pairs]
        valid = None
        if masked:
            key = lax.broadcasted_iota(jnp.int32, (blk, 2 * blk), 0)
            qry = lax.broadcasted_iota(jnp.int32, (blk, 2 * blk), 1)
            valid = key < jnp.where(qry >= blk, qry - blk, qry)
        logs = [_sb_logs(z[a], valid) for a in pairs]
        sums = [jnp.dot(mt_ref[...], logs[a][1], preferred_element_type=F32) for a in pairs]
        w = [_sb_local_weights(logs[a][0], sums[a][:blk], valid) for a in pairs]
        for a in pairs:
            carry = carry_ref[a]
            pv = jnp.dot(vt_ref[0, kb, cols[a]], w[a], preferred_element_type=F32)
            scaled = pv.reshape(PAIR // 8, 8, 2 * blk) * jnp.exp(carry)[None]
            acc_ref[a] += scaled.reshape(PAIR, 2 * blk)
            carry_ref[a] = carry + sums[a][blk:]

    step(i, True)

    def block(jb, carry):
        step(i - jb, False)
        return carry

    lax.fori_loop(1, i + 1, block, 0)
    for a in pairs:
        both = jnp.where(dim < HEAD_DIM, acc_ref[a, :, :blk], acc_ref[a, :, blk:])
        o_ref[0, :, cols[a]] = both.T.astype(BF16)


def _sb_prompt(q, k, v, bias, blk):
    bsz, tp, _ = q.shape
    nq = tp // blk
    n_par = ATT_PAIRS_PER_STEP
    wide = n_par * PAIR
    vt = v.reshape(bsz, nq, blk, WIDTH).transpose(0, 1, 3, 2)
    return pl.pallas_call(
        functools.partial(_sb_prompt_body, blk=blk, n_par=n_par),
        grid=(bsz, N_PAIRS // n_par, nq),
        in_specs=[pl.BlockSpec(memory_space=pltpu.SMEM),
                  pl.BlockSpec((1, blk, wide), lambda b, p, i: (b, i, p)),
                  pl.BlockSpec((1, tp, wide), lambda b, p, i: (b, 0, p)),
                  pl.BlockSpec((1, nq, wide, blk), lambda b, p, i: (b, 0, p, 0))],
        out_specs=pl.BlockSpec((1, blk, wide), lambda b, p, i: (b, i, p)),
        out_shape=jax.ShapeDtypeStruct((bsz, tp, WIDTH), BF16),
        scratch_shapes=[pltpu.VMEM((n_par, PAIR, 2 * blk), BF16), pltpu.VMEM((blk + 8, blk), BF16),
                        pltpu.VMEM((n_par, PAIR, 2 * blk), F32), pltpu.VMEM((n_par, 8, 2 * blk), F32)],
        compiler_params=_params(("parallel", "parallel", "arbitrary")),
        name="sb_prompt",
    )(bias, q, k, vt)


def _sb_paged_body(pt_ref, bias_ref, q_ref, kn_ref, vn_ref, *rest, tq, group, n_steps):
    del pt_ref
    k_refs = rest[:group]
    v_refs = rest[group:2 * group]
    o_ref, qbd_ref, bias_t_ref, uu_ref, acc_ref, carry_ref = rest[2 * group:]
    j = pl.program_id(1)
    rows = HEADS * tq

    def sweep(kts, vts, carry, acc, valid):
        n = len(kts)
        kt = jnp.concatenate([t.astype(BF16) for t in kts], axis=1)
        vt = jnp.concatenate([t.astype(BF16) for t in vts], axis=1)
        s = jnp.dot(qbd_ref[...], kt, preferred_element_type=F32)
        page = lambda a, g: a[:, g * PAGE_SIZE:(g + 1) * PAGE_SIZE]
        logs = [_sb_logs(page(s, g) + bias_t_ref[...], valid) for g in range(n)]
        rest, total = _sb_rest(jnp.concatenate([lg[1] for lg in logs], axis=0), uu_ref[...])
        w = []
        for g in range(n):
            rows_g = slice(g * rows, (g + 1) * rows)
            w.append(_sb_local_weights(logs[g][0] + carry, rest[rows_g], valid))
            carry = carry + total[rows_g]
        w = jnp.concatenate(w, axis=1)
        return carry, acc + lax.dot_general(w, vt, _NT, preferred_element_type=F32)

    @pl.when(j == 0)
    def _():
        rw = lax.broadcasted_iota(jnp.int32, (rows, WIDTH), 0)
        cw = lax.broadcasted_iota(jnp.int32, (rows, WIDTH), 1)
        q_all = jnp.concatenate([q_ref[0]] * HEADS, axis=0)
        qbd_ref[...] = jnp.where(rw // tq == cw // HEAD_DIM, q_all, 0.0).astype(BF16)
        row = lax.broadcasted_iota(jnp.int32, (rows, PAGE_SIZE), 0)
        col = lax.broadcasted_iota(jnp.int32, (rows, PAGE_SIZE), 1)
        bias_t = jnp.zeros((rows, PAGE_SIZE), F32)
        for h in range(HEADS):
            bias_t = jnp.where(row // tq == h, bias_ref[h], bias_t)
        bias_t_ref[...] = bias_t
        uu_ref[...] = _strict_upper(PAGE_SIZE)
        valid = col < row % tq
        carry, acc = sweep([kn_ref[0]], [vn_ref[0]], jnp.zeros((rows, 1), F32), jnp.zeros((rows, WIDTH), F32), valid)
        carry_ref[...] = carry
        acc_ref[...] = acc

    carry, acc = sweep([r[0, 0] for r in k_refs], [r[0, 0] for r in v_refs], carry_ref[...], acc_ref[...], None)
    carry_ref[...] = carry
    acc_ref[...] = acc

    @pl.when(j == n_steps - 1)
    def _():
        rw = lax.broadcasted_iota(jnp.int32, (tq, WIDTH), 1)
        out = jnp.zeros((tq, WIDTH), F32)
        for h in range(HEADS):
            out = jnp.where(rw // HEAD_DIM == h, acc_ref[h * tq:(h + 1) * tq], out)
        o_ref[0] = out


def _sb_paged(q, kt_new, vt_new, cache_kt, cache_vt, page_table, bias, layer):
    bsz, tq, _ = q.shape
    n_pages = page_table.shape[1]
    group = next(g for g in (PAGE_GROUP, 4, 2, 1) if n_pages % g == 0)
    n_steps = n_pages // group
    rows = HEADS * tq
    per_b = lambda b, j, pt: (b, 0, 0)

    def page(g):
        return lambda b, j, pt: (layer, pt[b, n_pages - 1 - (j * group + g)], 0, 0)

    page_specs = [pl.BlockSpec((1, 1, WIDTH, PAGE_SIZE), page(g)) for g in range(group)]
    grid_spec = pltpu.PrefetchScalarGridSpec(
        num_scalar_prefetch=1,
        grid=(bsz, n_steps),
        in_specs=[pl.BlockSpec(memory_space=pltpu.SMEM),
                  pl.BlockSpec((1, tq, WIDTH), per_b),
                  pl.BlockSpec((1, WIDTH, PAGE_SIZE), per_b),
                  pl.BlockSpec((1, WIDTH, PAGE_SIZE), per_b)] + page_specs + page_specs,
        out_specs=pl.BlockSpec((1, tq, WIDTH), per_b),
        scratch_shapes=[pltpu.VMEM((rows, WIDTH), BF16), pltpu.VMEM((rows, PAGE_SIZE), F32),
                        pltpu.VMEM((PAGE_SIZE, PAGE_SIZE), BF16),
                        pltpu.VMEM((rows, WIDTH), F32), pltpu.VMEM((rows, 1), F32)])
    return pl.pallas_call(
        functools.partial(_sb_paged_body, tq=tq, group=group, n_steps=n_steps),
        grid_spec=grid_spec,
        out_shape=jax.ShapeDtypeStruct((bsz, tq, WIDTH), F32),
        compiler_params=_params(("parallel", "arbitrary")),
        name="sb_paged",
    )(page_table, bias, q, kt_new, vt_new, *([cache_kt] * group), *([cache_vt] * group))


def _mlstm_body(q_ref, k_ref, v_ref, og_ref, gc_ref, bc_ref, hg_ref, c0_ref, n0_ref, m0_ref,
                y_ref, c_out_ref, n_out_ref, m_out_ref, c_s, n_s, m_s, *, chunk, n_seq, n_tok, n_real, n_chunks):
    ci = pl.program_id(1)

    @pl.when(ci == 0)
    def _():
        c_s[...] = c0_ref[...]
        n_s[...] = n0_ref[...]
        m_s[...] = m0_ref[...]

    L = chunk
    short = n_tok < L

    def tokens(ref, b, cols=slice(None)):
        x = ref[b, :, cols]
        if short:
            x = jnp.concatenate([x, jnp.zeros((L - n_tok, x.shape[1]), x.dtype)], axis=0)
        return x

    r_i = lax.broadcasted_iota(jnp.int32, (L, L), 0)
    c_i = lax.broadcasted_iota(jnp.int32, (L, L), 1)
    causal = r_i >= c_i
    tri = jnp.where(causal, 1.0, 0.0).astype(F32)
    lane_g = lax.broadcasted_iota(jnp.int32, (L, GATE_LANES), 1)
    lane = lax.broadcasted_iota(jnp.int32, (L, PAIR), 1)
    lane1 = lax.broadcasted_iota(jnp.int32, (1, PAIR), 1)
    blk_r = lax.broadcasted_iota(jnp.int32, (PAIR, PAIR), 0)
    blk_c = lax.broadcasted_iota(jnp.int32, (PAIR, PAIR), 1)
    same_head = (blk_r < HEAD_DIM) == (blk_c < HEAD_DIM)
    lane_m = lax.broadcasted_iota(jnp.int32, (1, HEADS), 1)
    mine = [lane < HEAD_DIM, lane >= HEAD_DIM]
    cols = [slice(p * PAIR, (p + 1) * PAIR) for p in range(N_PAIRS)]
    seqs = range(n_seq)
    probs = [(b, h) for b in seqs for h in range(HEADS)]
    pair_ids = [(b, p) for b in seqs for p in range(N_PAIRS)]

    x_c, cum_c, x_r, cum_r = {}, {}, {}, {}
    for b in seqs:
        gate_c = tokens(gc_ref, b) + bc_ref[...]
        xc = jnp.where(lane_g >= HEADS, _log_sigmoid(gate_c), gate_c)
        if n_real < n_chunks * L:
            tok_c = ci * L + lax.broadcasted_iota(jnp.int32, (L, GATE_LANES), 0)
            xc = jnp.where(tok_c < n_real, xc, jnp.where(lane_g >= HEADS, 0.0, NEG_BIG))
        cc = jnp.dot(tri, xc, preferred_element_type=F32, precision=lax.Precision.HIGHEST)
        x_c[b], cum_c[b] = xc, cc
        x_r[b] = xc.T[:2 * HEADS]
        cum_r[b] = cc.T[:2 * HEADS]

    q2 = {bp: tokens(q_ref, bp[0], cols[bp[1]]) for bp in pair_ids}
    k2 = {bp: tokens(k_ref, bp[0], cols[bp[1]]) for bp in pair_ids}
    k2b = {bp: k2[bp].astype(BF16) for bp in pair_ids}
    v2b = {bp: tokens(v_ref, bp[0], cols[bp[1]]).astype(BF16) for bp in pair_ids}
    c_pair = {bp: c_s[bp[0], bp[1]] for bp in pair_ids}
    n_row = {bp: n_s[bp[0], bp[1]] for bp in pair_ids}
    pair_of = lambda bh: (bh[0], bh[1] // 2)
    qm = {bh: jnp.where(mine[bh[1] % 2], q2[pair_of(bh)], 0.0) for bh in probs}
    qmb = {bh: qm[bh].astype(BF16) for bh in probs}

    b_c, i_c, b_r, i_r, m_prev, m_new, wend, decay = {}, {}, {}, {}, {}, {}, {}, {}
    for bh in probs:
        b, h = bh
        b_c[bh] = cum_c[b][:, HEADS + h:HEADS + h + 1]
        i_c[bh] = x_c[b][:, h:h + 1]
        b_r[bh] = cum_r[b][HEADS + h:HEADS + h + 1, :]
        i_r[bh] = x_r[b][h:h + 1, :]
        m_prev[bh] = m_s[b][:, h:h + 1]
        b_last = b_r[bh][:, L - 1:L]
        m_new[bh] = jnp.maximum(b_last + m_prev[bh],
                                jnp.max(b_last - b_r[bh] + i_r[bh], axis=1, keepdims=True))
        wend[bh] = jnp.exp(b_last - b_c[bh] + i_c[bh] - m_new[bh])
        decay[bh] = jnp.exp(b_last + m_prev[bh] - m_new[bh])
    kw = {(b, p): k2[(b, p)] * jnp.where(mine[1], wend[(b, 2 * p + 1)], wend[(b, 2 * p)]) for b, p in pair_ids}

    qk = {bh: lax.dot_general(qmb[bh], k2b[pair_of(bh)], _NT, preferred_element_type=F32) for bh in probs}
    qc = {bh: jnp.dot(qmb[bh], c_pair[pair_of(bh)].astype(BF16), preferred_element_type=F32) for bh in probs}
    upd = {bp: jnp.dot(kw[bp].T.astype(BF16), v2b[bp], preferred_element_type=F32) for bp in pair_ids}

    s, w_inter, m_t = {}, {}, {}
    for bh in probs:
        log_d = jnp.where(causal, b_c[bh] - b_r[bh] + i_r[bh], NEG_BIG)
        inter = b_c[bh] + m_prev[bh]
        m_t[bh] = jnp.maximum(inter, jnp.max(log_d, axis=1, keepdims=True))
        s[bh] = qk[bh] * jnp.exp(log_d - m_t[bh])
        w_inter[bh] = jnp.exp(inter - m_t[bh])
    sv = {bh: jnp.dot(s[bh].astype(BF16), v2b[pair_of(bh)], preferred_element_type=F32) for bh in probs}

    hid = {}
    for bh in probs:
        num = sv[bh] + w_inter[bh] * qc[bh]
        den = (jnp.sum(s[bh], axis=1, keepdims=True)
               + w_inter[bh] * jnp.sum(qm[bh] * n_row[pair_of(bh)], axis=1, keepdims=True))
        x = num * (1.0 / jnp.maximum(jnp.abs(den), jnp.exp(-m_t[bh])))
        ms = jnp.sum(jnp.where(mine[bh[1] % 2], x * x, 0.0), axis=1, keepdims=True) * (1.0 / HEAD_DIM)
        hid[bh] = x * lax.rsqrt(ms + RMS_EPS)

    for b in seqs:
        m_next = m_s[b]
        for h in range(HEADS):
            m_next = jnp.where(lane_m == h, m_new[(b, h)], m_next)
        m_s[b] = m_next
    for b, p in pair_ids:
        h_pair = jnp.where(mine[1], hid[(b, 2 * p + 1)], hid[(b, 2 * p)])
        y = tokens(og_ref, b, cols[p]) * h_pair * hg_ref[:, cols[p]]
        y_ref[b, :, cols[p]] = y[:n_tok].astype(y_ref.dtype) if short else y.astype(y_ref.dtype)
        decay_pair = jnp.where(lane1 >= HEAD_DIM, decay[(b, 2 * p + 1)], decay[(b, 2 * p)])
        c_s[b, p] = decay_pair * c_pair[(b, p)] + jnp.where(same_head, upd[(b, p)], 0.0)
        n_s[b, p] = decay_pair * n_row[(b, p)] + jnp.sum(kw[(b, p)], axis=0, keepdims=True)

    @pl.when(ci == n_chunks - 1)
    def _():
        c_out_ref[...] = c_s[...]
        n_out_ref[...] = n_s[...]
        m_out_ref[...] = m_s[...]


def _pack_pairs(c):
    bsz = c.shape[0]
    c = c.reshape(bsz, N_PAIRS, 2, HEAD_DIM, HEAD_DIM)
    z = jnp.zeros_like(c[:, :, 0])
    top = jnp.concatenate([c[:, :, 0], z], axis=-1)
    bot = jnp.concatenate([z, c[:, :, 1]], axis=-1)
    return jnp.concatenate([top, bot], axis=-2)


def _unpack_pairs(cp):
    bsz = cp.shape[0]
    a = cp[:, :, :HEAD_DIM, :HEAD_DIM]
    b = cp[:, :, HEAD_DIM:, HEAD_DIM:]
    return jnp.stack([a, b], axis=2).reshape(bsz, HEADS, HEAD_DIM, HEAD_DIM)


def _mlstm(q, k, v, og, gates, bias_c, head_gain, c0, n0, m0, n_real):
    bsz, t, _ = q.shape
    L = ML_CHUNK
    n_chunks = max(t // L, 1)
    tb = min(t, L)
    assert t == n_chunks * tb
    nb = ML_SEQS_PER_STEP if bsz % ML_SEQS_PER_STEP == 0 else 1
    y_dtype = BF16 if tb == L else F32
    tok = lambda b, c: (b, c, 0)
    per_b3 = lambda b, c: (b, 0, 0)
    per_b4 = lambda b, c: (b, 0, 0, 0)
    y, c_out, n_out, m_out = pl.pallas_call(
        functools.partial(_mlstm_body, chunk=L, n_seq=nb, n_tok=tb, n_real=n_real, n_chunks=n_chunks),
        grid=(bsz // nb, n_chunks),
        in_specs=[pl.BlockSpec((nb, tb, WIDTH), tok)] * 4 + [
            pl.BlockSpec((nb, tb, GATE_LANES), tok),
            pl.BlockSpec((1, GATE_LANES), lambda b, c: (0, 0)),
            pl.BlockSpec((1, WIDTH), lambda b, c: (0, 0)),
            pl.BlockSpec((nb, N_PAIRS, PAIR, PAIR), per_b4),
            pl.BlockSpec((nb, N_PAIRS, 1, PAIR), per_b4),
            pl.BlockSpec((nb, 1, HEADS), per_b3)],
        out_specs=[pl.BlockSpec((nb, tb, WIDTH), tok),
                   pl.BlockSpec((nb, N_PAIRS, PAIR, PAIR), per_b4),
                   pl.BlockSpec((nb, N_PAIRS, 1, PAIR), per_b4),
                   pl.BlockSpec((nb, 1, HEADS), per_b3)],
        out_shape=[jax.ShapeDtypeStruct((bsz, t, WIDTH), y_dtype),
                   jax.ShapeDtypeStruct((bsz, N_PAIRS, PAIR, PAIR), F32),
                   jax.ShapeDtypeStruct((bsz, N_PAIRS, 1, PAIR), F32),
                   jax.ShapeDtypeStruct((bsz, 1, HEADS), F32)],
        scratch_shapes=[pltpu.VMEM((nb, N_PAIRS, PAIR, PAIR), F32), pltpu.VMEM((nb, N_PAIRS, 1, PAIR), F32),
                        pltpu.VMEM((nb, 1, HEADS), F32)],
        compiler_params=_params(("parallel", "arbitrary")),
        name="mlstm",
    )(q, k, v, og, gates, bias_c, head_gain,
      _pack_pairs(c0), n0.reshape(bsz, N_PAIRS, 1, PAIR), m0.reshape(bsz, 1, HEADS))
    return (y, _unpack_pairs(c_out), n_out.reshape(bsz, HEADS, HEAD_DIM), m_out.reshape(bsz, HEADS))


def _merge_body(*refs, conv_here):
    if conv_here:
        x_ref, ysb_ref, yml_ref, cb_ref, u_ref, u1_ref, u2_ref, mg_ref, cw_ref, wbr_ref, wout_ref, o_ref = refs
        cw = cw_ref[...]
        conv = cw[0:1] * u2_ref[...] + cw[1:2] * u1_ref[...] + cw[2:3] * u_ref[...]
        ycv = (cb_ref[...] * conv).astype(BF16)
    else:
        x_ref, ysb_ref, yml_ref, ycv_ref, mg_ref, wbr_ref, wout_ref, o_ref = refs
        ycv = ycv_ref[...]
    gate = lambda j: mg_ref[:, j * D_MODEL:(j + 1) * D_MODEL].astype(F32)
    merged = (gate(0) * jnp.dot(ysb_ref[...], wbr_ref[0], preferred_element_type=F32)
              + gate(1) * jnp.dot(yml_ref[...], wbr_ref[1], preferred_element_type=F32)
              + gate(2) * jnp.dot(ycv, wbr_ref[2], preferred_element_type=F32))
    o_ref[...] = x_ref[...] + jnp.dot(merged.astype(BF16), wout_ref[...], preferred_element_type=F32)


def _merge(x, ysb, yml, conv_inputs, mg, w_br, w_out):
    m = x.shape[0]
    tm = _row_tile(m, ROW_TILE_PROJ)
    row = lambda i: (i, 0)
    wide = pl.BlockSpec((tm, WIDTH), row)
    conv_here = len(conv_inputs) > 1
    conv_specs = [wide] * 4 if conv_here else [wide]
    weights = ([conv_inputs[4]] if conv_here else []) + [w_br, w_out]
    return pl.pallas_call(
        functools.partial(_merge_body, conv_here=conv_here),
        grid=(m // tm,),
        in_specs=[pl.BlockSpec((tm, D_MODEL), row), wide, wide] + conv_specs + [
            pl.BlockSpec((tm, 3 * D_MODEL), row)] + [_resident(w.shape) for w in weights],
        out_specs=pl.BlockSpec((tm, D_MODEL), row),
        out_shape=jax.ShapeDtypeStruct((m, D_MODEL), F32),
        compiler_params=_params(("parallel",)),
        name="merge",
    )(x, ysb, yml, *conv_inputs[:4], mg, *weights)


def _shifted(u, buf):
    ext = jnp.concatenate([buf, u], axis=1)
    t = u.shape[1]
    return ext[:, 1:t + 1], ext[:, 0:t]


def kernel(x_prompt, x_sample, cache_sb_k, cache_sb_v, page_table, state_ml_C, state_ml_n, state_ml_m,
           state_conv, meta_tokens, norm_ff1, ffn1_w_up, ffn1_w_down, norm_mix, w_in, sb_logit_bias,
           ml_igate_bias, ml_fgate_bias, ml_head_norm, conv_w, w_branch_sb, w_branch_ml, w_branch_cv, w_out,
           norm_ff2, ffn2_w_up, ffn2_w_down, norm_final):
    bp, seq, _ = x_prompt.shape
    bs, ts, _ = x_sample.shape
    depth = w_in.shape[0]
    d_ff = ffn1_w_down.shape[1]
    n_ff = d_ff // FF_CHUNK
    n_small = bs * ts
    blk = ATT_BLOCK
    nt = N_META + seq
    tf = -(-nt // blk) * blk
    assert blk % ML_CHUNK == 0 and d_ff % FF_CHUNK == 0
    assert ts % 8 == 0 and ts <= ML_CHUNK

    def ffn_weights(w_up, w_down):
        gate = w_up[:, :d_ff].reshape(D_MODEL, n_ff, FF_CHUNK)
        up = w_up[:, d_ff:].reshape(D_MODEL, n_ff, FF_CHUNK)
        wu = jnp.concatenate([gate, up], axis=-1).transpose(1, 0, 2).astype(BF16)
        return wu, w_down.reshape(n_ff, FF_CHUNK, D_MODEL).astype(BF16)

    def in_weights(w):
        main = jnp.concatenate([w[:, :6 * WIDTH], w[:, 6 * WIDTH + 2 * HEADS:]], axis=1)
        w_seg = main.reshape(D_MODEL, N_SEG, WIDTH).transpose(1, 0, 2).astype(BF16)
        gate = w[:, 6 * WIDTH:6 * WIDTH + 2 * HEADS]
        w_gate = jnp.pad(gate, ((0, 0), (0, GATE_LANES - 2 * HEADS))).astype(BF16)
        return w_seg, w_gate

    row1 = lambda a: a.reshape(1, -1)
    meta = jnp.broadcast_to(meta_tokens.astype(F32)[None], (bp, N_META, D_MODEL))
    xb = jnp.concatenate([meta, x_prompt, jnp.zeros((bp, tf - nt, D_MODEL), F32)], axis=1).reshape(bp * tf, D_MODEL)
    xs = x_sample.reshape(n_small, D_MODEL)
    pages_t = lambda c: jnp.transpose(c, (0, 1, 3, 4, 2)).reshape(c.shape[:2] + (WIDTH, PAGE_SIZE))
    cache_kt = pages_t(cache_sb_k)
    cache_vt = pages_t(cache_sb_v)
    frame = lambda a: a.reshape(bp, tf, a.shape[-1])
    smp = lambda a: a.reshape(bs, ts, a.shape[-1])
    flat = lambda a: a.reshape(-1, a.shape[-1])
    heads = lambda a: a.reshape(a.shape[:-1] + (HEADS, HEAD_DIM))

    new_p, new_s = [], []
    for l in range(depth):
        last = l == depth - 1
        wu1, wd1 = ffn_weights(ffn1_w_up[l], ffn1_w_down[l])
        wu2, wd2 = ffn_weights(ffn2_w_up[l], ffn2_w_down[l])
        w_seg, w_gate = in_weights(w_in[l])
        w_br = jnp.stack([w_branch_sb[l], w_branch_ml[l], w_branch_cv[l]]).astype(BF16)
        w_o = w_out[l].astype(BF16)
        bias_c = jnp.pad(jnp.concatenate([ml_igate_bias[l], ml_fgate_bias[l]]),
                         (0, GATE_LANES - 2 * HEADS)).reshape(1, GATE_LANES)
        ml_args = (bias_c, row1(ml_head_norm[l]))

        xb = _ffn(xb, row1(norm_ff1[l]), wu1, wd1)
        xs = _ffn(xs, row1(norm_ff1[l]), wu1, wd1)
        (qa_b, kn_b, ka_b, vn_b, va_b, mq_b, mk_b, mv_b, og_b, ycv_b, ulast_b, mg_b, ig_b) = _in_proj(
            xb, row1(norm_mix[l]), w_seg, w_gate, conv_w[l], (bp, tf, nt))
        (qa_s, kn_s, ka_s, vn_s, va_s, mq_s, mk_s, mv_s, og_s, cb_s, u_s, mg_s, ig_s) = _in_proj(
            xs, row1(norm_mix[l]), w_seg, w_gate)

        ysb_b = flat(_sb_prompt(frame(qa_b), frame(ka_b), frame(va_b), sb_logit_bias[l], blk))
        new_t = lambda a: jnp.pad(jnp.swapaxes(smp(a), 1, 2), ((0, 0), (0, 0), (0, PAGE_SIZE - ts)))
        ysb_s = flat(_sb_paged(smp(qa_s).astype(F32), new_t(kn_s), new_t(vn_s), cache_kt, cache_vt, page_table,
                               sb_logit_bias[l], l)).astype(BF16)

        yml_b, c_p, n_p, m_p = _mlstm(
            frame(mq_b), frame(mk_b), frame(mv_b), frame(og_b), frame(ig_b), *ml_args,
            jnp.zeros((bp, HEADS, HEAD_DIM, HEAD_DIM), F32), jnp.zeros((bp, HEADS, HEAD_DIM), F32),
            jnp.zeros((bp, HEADS), F32), nt)
        yml_s, c_s, n_s, m_s = _mlstm(
            smp(mq_s), smp(mk_s), smp(mv_s), smp(og_s), smp(ig_s), *ml_args,
            state_ml_C[l].astype(F32), state_ml_n[l].astype(F32), state_ml_m[l].astype(F32), ts)

        u_smp = smp(u_s)
        u1_s, u2_s = _shifted(u_smp, state_conv[l].astype(F32))
        xb = _merge(xb, ysb_b, flat(yml_b), (ycv_b,), mg_b, w_br, w_o)
        xs = _merge(xs, ysb_s, flat(yml_s).astype(BF16), (cb_s, u_s, flat(u1_s), flat(u2_s), conv_w[l]), mg_s,
                    w_br, w_o)

        fin = row1(norm_final) if last else None
        xb = _ffn(xb, row1(norm_ff2[l]), wu2, wd2, fin)
        xs = _ffn(xs, row1(norm_ff2[l]), wu2, wd2, fin)

        new_p.append((heads(kn_b), heads(vn_b), c_p, n_p, m_p, ulast_b[:, 6:]))
        new_s.append((heads(smp(kn_s)), heads(smp(vn_s)), c_s, n_s, m_s,
                      jnp.concatenate([state_conv[l].astype(F32), u_smp], axis=1)[:, -2:]))

    y_prompt = frame(xb)[:, N_META:nt]
    y_sample = xs.reshape(bs, ts, D_MODEL)
    stk = lambda states, i: jnp.stack([st[i] for st in states])
    return (y_prompt, y_sample) + tuple(stk(new_p, i) for i in range(6)) + tuple(stk(new_s, i) for i in range(6))
```

```python
import functools

import jax
import jax.numpy as jnp
from jax import lax
from jax.experimental import pallas as pl
from jax.experimental.pallas import tpu as pltpu

F32 = jnp.float32
BF16 = jnp.bfloat16

D_MODEL = 1024
N_META = 16
HEADS = 8
HEAD_DIM = 64
WIDTH = HEADS * HEAD_DIM
PAIR = 2 * HEAD_DIM
N_PAIRS = HEADS // 2
PAGE_SIZE = 128
RMS_EPS = 1e-6
QK_SCALE = HEAD_DIM ** -0.5
NEG_BIG = -1e30
LOG2E = 1.4426950408889634

FF_CHUNK = 256
ROW_TILE_FFN = 512
ROW_TILE_PROJ = 256
ATT_BLOCK = 256
ATT_PAIRS_PER_STEP = 4
ML_CHUNK = 128
ML_SEQS_PER_STEP = 2
PAGE_GROUP = 32
N_SEG = 16
GATE_LANES = 128
VMEM_LIMIT = 56 * 1024 * 1024

_NT = (((1,), (1,)), ((), ()))


def _params(semantics):
    return pltpu.CompilerParams(dimension_semantics=semantics, vmem_limit_bytes=VMEM_LIMIT)


def _resident(shape):
    nd = len(shape)
    return pl.BlockSpec(shape, lambda *_: (0,) * nd, pipeline_mode=pl.Buffered(1))


def _rms(x, gain):
    return x * lax.rsqrt(jnp.mean(x * x, axis=-1, keepdims=True) + RMS_EPS) * gain


def _log_sigmoid(x):
    return jnp.minimum(x, 0.0) - jnp.log1p(jnp.exp(-jnp.abs(x)))


def _row_tile(m, target):
    for t in (target, target // 2, target // 4):
        if m % t == 0:
            return t
    return m


def _ffn_body(*refs, n_chunks, final):
    if final:
        x_ref, g_ref, wup_ref, wdn_ref, gf_ref, o_ref, xn_ref, acc_ref = refs
    else:
        x_ref, g_ref, wup_ref, wdn_ref, o_ref, xn_ref, acc_ref = refs
    xn_ref[...] = _rms(x_ref[...], g_ref[...]).astype(BF16)
    acc_ref[...] = jnp.zeros_like(acc_ref)

    def chunk(c, carry):
        gu = jnp.dot(xn_ref[...], wup_ref[c], preferred_element_type=F32)
        gate, up = gu[:, :FF_CHUNK], gu[:, FF_CHUNK:]
        act = (gate * jax.nn.sigmoid(gate) * up).astype(BF16)
        acc_ref[...] += jnp.dot(act, wdn_ref[c], preferred_element_type=F32)
        return carry

    lax.fori_loop(0, n_chunks, chunk, 0, unroll=True)
    y = x_ref[...] + 0.5 * acc_ref[...]
    if final:
        y = _rms(y, gf_ref[...])
    o_ref[...] = y


def _ffn(x, gain, w_up, w_down, final_gain=None):
    m = x.shape[0]
    tm = _row_tile(m, ROW_TILE_FFN)
    n_chunks = w_up.shape[0]
    final = final_gain is not None
    row = lambda i: (i, 0)
    in_specs = [pl.BlockSpec((tm, D_MODEL), row), _resident((1, D_MODEL)),
                _resident(w_up.shape), _resident(w_down.shape)]
    args = [x, gain, w_up, w_down]
    if final:
        in_specs.append(_resident((1, D_MODEL)))
        args.append(final_gain)
    return pl.pallas_call(
        functools.partial(_ffn_body, n_chunks=n_chunks, final=final),
        grid=(m // tm,),
        in_specs=in_specs,
        out_specs=pl.BlockSpec((tm, D_MODEL), row),
        out_shape=jax.ShapeDtypeStruct((m, D_MODEL), F32),
        scratch_shapes=[pltpu.VMEM((tm, D_MODEL), BF16), pltpu.VMEM((tm, D_MODEL), F32)],
        compiler_params=_params(("parallel",)),
        name="ffn",
    )(*args)


def _inproj_body(*refs, frame):
    if frame is None:
        (x_ref, g_ref, wseg_ref, wgate_ref, qa_ref, kn_ref, ka_ref, vn_ref, va_ref, mq_ref, mk_ref, mv_ref,
         mo_ref, cb_ref, u_ref, mg_ref, ig_ref, xn_ref) = refs
    else:
        (x_ref, g_ref, wseg_ref, wgate_ref, cw_ref, qa_ref, kn_ref, ka_ref, vn_ref, va_ref, mq_ref, mk_ref, mv_ref,
         mo_ref, ycv_ref, ulast_ref, mg_ref, ig_ref, xn_ref, tail_ref) = refs
    xn_ref[...] = _rms(x_ref[...], g_ref[...]).astype(BF16)
    seg = lambda s: jnp.dot(xn_ref[...], wseg_ref[s], preferred_element_type=F32)
    qa_ref[...] = (seg(0) * QK_SCALE).astype(BF16)
    k = seg(1)
    kn_ref[...] = k.reshape(kn_ref.shape)
    ka_ref[...] = k.astype(BF16)
    v = seg(2)
    vn_ref[...] = v.reshape(vn_ref.shape)
    va_ref[...] = v.astype(BF16)
    mq_ref[...] = seg(3)
    mk_ref[...] = seg(4) * QK_SCALE
    mv_ref[...] = seg(5)
    mo_ref[...] = jax.nn.sigmoid(seg(6))
    u = seg(8) * seg(9)
    if frame is None:
        cb_ref[...] = seg(7)
        u_ref[...] = u
    else:
        tiles_per_seq, last_tile, last_row = frame
        tm = u.shape[0]
        t = pl.program_id(0) % tiles_per_seq

        @pl.when(t == 0)
        def _():
            tail_ref[...] = jnp.zeros_like(tail_ref)

        ext = jnp.concatenate([tail_ref[...], u], axis=0)
        cw = cw_ref[...]
        conv = cw[0:1] * ext[6:tm + 6] + cw[1:2] * ext[7:tm + 7] + cw[2:3] * u
        ycv_ref[...] = (seg(7) * conv).astype(BF16)
        tail_ref[...] = u[tm - 8:]

        @pl.when(t == last_tile)
        def _():
            ulast_ref[0] = u[last_row:last_row + 8]
    for s in range(6):
        mg_ref[:, s * WIDTH:(s + 1) * WIDTH] = jax.nn.sigmoid(seg(10 + s)).astype(BF16)
    ig_ref[...] = jnp.dot(xn_ref[...], wgate_ref[...], preferred_element_type=F32)


def _in_proj(x, gain, w_seg, w_gate, conv_w=None, frame=None):
    m = x.shape[0]
    tm = _row_tile(m, ROW_TILE_PROJ)
    row = lambda i: (i, 0)
    wide = lambda dt: jax.ShapeDtypeStruct((m, WIDTH), dt)
    wide_spec = pl.BlockSpec((tm, WIDTH), row)
    tail_shape = [jax.ShapeDtypeStruct((m, 3 * D_MODEL), BF16), jax.ShapeDtypeStruct((m, GATE_LANES), F32)]
    tail_specs = [pl.BlockSpec((tm, 3 * D_MODEL), row), pl.BlockSpec((tm, GATE_LANES), row)]
    in_specs = [pl.BlockSpec((tm, D_MODEL), row), _resident((1, D_MODEL)),
                _resident(w_seg.shape), _resident(w_gate.shape)]
    scratch = [pltpu.VMEM((tm, D_MODEL), BF16)]
    args = [x, gain, w_seg, w_gate]
    if frame is None:
        out_shape = [wide(BF16), wide(F32), wide(BF16), wide(F32), wide(BF16),
                     wide(F32), wide(F32), wide(F32), wide(F32), wide(F32), wide(F32)] + tail_shape
        out_specs = [wide_spec] * 11 + tail_specs
        body_frame = None
    else:
        n_seq, tf, nt = frame
        assert tf % tm == 0 and nt % 8 == 0
        tps = tf // tm
        kv_shape = jax.ShapeDtypeStruct((n_seq, nt, WIDTH), F32)
        kv_spec = pl.BlockSpec((1, tm, WIDTH), lambda i: (i // tps, i % tps, 0))
        out_shape = [wide(BF16), kv_shape, wide(BF16), kv_shape, wide(BF16), wide(F32), wide(F32), wide(F32),
                     wide(F32), wide(BF16), jax.ShapeDtypeStruct((n_seq, 8, WIDTH), F32)] + tail_shape
        out_specs = [wide_spec, kv_spec, wide_spec, kv_spec, wide_spec, wide_spec, wide_spec, wide_spec,
                     wide_spec, wide_spec, pl.BlockSpec((1, 8, WIDTH), lambda i: (i // tps, 0, 0))] + tail_specs
        in_specs.append(_resident(conv_w.shape))
        args.append(conv_w)
        scratch.append(pltpu.VMEM((8, WIDTH), F32))
        body_frame = (tps, (nt - 8) // tm, (nt - 8) % tm)
    return pl.pallas_call(
        functools.partial(_inproj_body, frame=body_frame),
        grid=(m // tm,),
        in_specs=in_specs,
        out_specs=out_specs,
        out_shape=out_shape,
        scratch_shapes=scratch,
        compiler_params=_params(("arbitrary",)),
        name="in_proj",
    )(*args)


def _strict_upper(blk):
    r = lax.broadcasted_iota(jnp.int32, (blk, blk), 0)
    c = lax.broadcasted_iota(jnp.int32, (blk, blk), 1)
    return jnp.where(r > c, 1.0, 0.0).astype(BF16)


def _sb_logs(z, valid):
    soft = jnp.log(1.0 + jnp.exp2(jnp.abs(z) * (-LOG2E)))
    log_beta = jnp.minimum(z, 0.0) - soft
    log_keep = log_beta - z
    if valid is not None:
        log_keep = jnp.where(valid, log_keep, 0.0)
    return log_beta, log_keep.astype(BF16)


def _sb_rest(log_keep, upper):
    rest = jnp.dot(log_keep, upper, preferred_element_type=F32)
    return rest, rest[:, :1] + log_keep[:, :1].astype(F32)


def _sb_local_weights(log_beta, rest, valid):
    w = jnp.exp(log_beta + rest)
    if valid is not None:
        w = jnp.where(valid, w, 0.0)
    return w.astype(BF16)


def _sb_prompt_body(bias_ref, q_ref, k_ref, vt_ref, o_ref, qt_ref, mt_ref, acc_ref, carry_ref, *, blk, n_par):
    first_pair = pl.program_id(1) * n_par
    i = pl.program_id(2)
    pairs = range(n_par)
    cols = [slice(a * PAIR, (a + 1) * PAIR) for a in pairs]
    dim = lax.broadcasted_iota(jnp.int32, (PAIR, blk), 0)
    for a in pairs:
        qt = q_ref[0, :, cols[a]].astype(F32).T
        qt_ref[a, :, :blk] = jnp.where(dim < HEAD_DIM, qt, 0.0).astype(BF16)
        qt_ref[a, :, blk:] = jnp.where(dim >= HEAD_DIM, qt, 0.0).astype(BF16)
    r = lax.broadcasted_iota(jnp.int32, (blk + 8, blk), 0)
    c = lax.broadcasted_iota(jnp.int32, (blk + 8, blk), 1)
    mt_ref[...] = jnp.where(jnp.logical_or(c > r, r >= blk), 1.0, 0.0).astype(BF16)
    acc_ref[...] = jnp.zeros_like(acc_ref)
    carry_ref[...] = jnp.zeros_like(carry_ref)
    second = lax.broadcasted_iota(jnp.int32, (1, 2 * blk), 1) >= blk
    bias = [jnp.where(second, bias_ref[2 * (first_pair + a) + 1], bias_ref[2 * (first_pair + a)]) for a in pairs]

    def step(kb, masked):
        start = pl.multiple_of(kb * blk, blk)
        z = [jnp.dot(k_ref[0, pl.ds(start, blk), cols[a]], qt_ref[a], preferred_element_type=F32) + bias[a]
             for a in pairs]
        valid = None
        if masked:
            key = lax.broadcasted_iota(jnp.int32, (blk, 2 * blk), 0)
            qry = lax.broadcasted_iota(jnp.int32, (blk, 2 * blk), 1)
            valid = key < jnp.where(qry >= blk, qry - blk, qry)
        logs = [_sb_logs(z[a], valid) for a in pairs]
        sums = [jnp.dot(mt_ref[...], logs[a][1], preferred_element_type=F32) for a in pairs]
        w = [_sb_local_weights(logs[a][0], sums[a][:blk], valid) for a in pairs]
        for a in pairs:
            carry = carry_ref[a]
            pv = jnp.dot(vt_ref[0, kb, cols[a]], w[a], preferred_element_type=F32)
            scaled = pv.reshape(PAIR // 8, 8, 2 * blk) * jnp.exp(carry)[None]
            acc_ref[a] += scaled.reshape(PAIR, 2 * blk)
            carry_ref[a] = carry + sums[a][blk:]

    step(i, True)

    def block(jb, carry):
        step(i - jb, False)
        return carry

    lax.fori_loop(1, i + 1, block, 0)
    for a in pairs:
        both = jnp.where(dim < HEAD_DIM, acc_ref[a, :, :blk], acc_ref[a, :, blk:])
        o_ref[0, :, cols[a]] = both.T.astype(BF16)


def _sb_prompt(q, k, v, bias, blk):
    bsz, tp, _ = q.shape
    nq = tp // blk
    n_par = ATT_PAIRS_PER_STEP
    wide = n_par * PAIR
    vt = v.reshape(bsz, nq, blk, WIDTH).transpose(0, 1, 3, 2)
    return pl.pallas_call(
        functools.partial(_sb_prompt_body, blk=blk, n_par=n_par),
        grid=(bsz, N_PAIRS // n_par, nq),
        in_specs=[pl.BlockSpec(memory_space=pltpu.SMEM),
                  pl.BlockSpec((1, blk, wide), lambda b, p, i: (b, i, p)),
                  pl.BlockSpec((1, tp, wide), lambda b, p, i: (b, 0, p)),
                  pl.BlockSpec((1, nq, wide, blk), lambda b, p, i: (b, 0, p, 0))],
        out_specs=pl.BlockSpec((1, blk, wide), lambda b, p, i: (b, i, p)),
        out_shape=jax.ShapeDtypeStruct((bsz, tp, WIDTH), BF16),
        scratch_shapes=[pltpu.VMEM((n_par, PAIR, 2 * blk), BF16), pltpu.VMEM((blk + 8, blk), BF16),
                        pltpu.VMEM((n_par, PAIR, 2 * blk), F32), pltpu.VMEM((n_par, 8, 2 * blk), F32)],
        compiler_params=_params(("parallel", "parallel", "arbitrary")),
        name="sb_prompt",
    )(bias, q, k, vt)


def _sb_paged_body(pt_ref, bias_ref, q_ref, kn_ref, vn_ref, *rest, tq, group, n_steps):
    del pt_ref
    k_refs = rest[:group]
    v_refs = rest[group:2 * group]
    o_ref, qbd_ref, bias_t_ref, uu_ref, acc_ref, carry_ref = rest[2 * group:]
    j = pl.program_id(1)
    rows = HEADS * tq

    def sweep(kts, vts, carry, acc, valid):
        n = len(kts)
        kt = jnp.concatenate([t.astype(BF16) for t in kts], axis=1)
        vt = jnp.concatenate([t.astype(BF16) for t in vts], axis=1)
        s = jnp.dot(qbd_ref[...], kt, preferred_element_type=F32)
        page = lambda a, g: a[:, g * PAGE_SIZE:(g + 1) * PAGE_SIZE]
        logs = [_sb_logs(page(s, g) + bias_t_ref[...], valid) for g in range(n)]
        rest, total = _sb_rest(jnp.concatenate([lg[1] for lg in logs], axis=0), uu_ref[...])
        w = []
        for g in range(n):
            rows_g = slice(g * rows, (g + 1) * rows)
            w.append(_sb_local_weights(logs[g][0] + carry, rest[rows_g], valid))
            carry = carry + total[rows_g]
        w = jnp.concatenate(w, axis=1)
        return carry, acc + lax.dot_general(w, vt, _NT, preferred_element_type=F32)

    @pl.when(j == 0)
    def _():
        rw = lax.broadcasted_iota(jnp.int32, (rows, WIDTH), 0)
        cw = lax.broadcasted_iota(jnp.int32, (rows, WIDTH), 1)
        q_all = jnp.concatenate([q_ref[0]] * HEADS, axis=0)
        qbd_ref[...] = jnp.where(rw // tq == cw // HEAD_DIM, q_all, 0.0).astype(BF16)
        row = lax.broadcasted_iota(jnp.int32, (rows, PAGE_SIZE), 0)
        col = lax.broadcasted_iota(jnp.int32, (rows, PAGE_SIZE), 1)
        bias_t = jnp.zeros((rows, PAGE_SIZE), F32)
        for h in range(HEADS):
            bias_t = jnp.where(row // tq == h, bias_ref[h], bias_t)
        bias_t_ref[...] = bias_t
        uu_ref[...] = _strict_upper(PAGE_SIZE)
        valid = col < row % tq
        carry, acc = sweep([kn_ref[0]], [vn_ref[0]], jnp.zeros((rows, 1), F32), jnp.zeros((rows, WIDTH), F32), valid)
        carry_ref[...] = carry
        acc_ref[...] = acc

    carry, acc = sweep([r[0, 0] for r in k_refs], [r[0, 0] for r in v_refs], carry_ref[...], acc_ref[...], None)
    carry_ref[...] = carry
    acc_ref[...] = acc

    @pl.when(j == n_steps - 1)
    def _():
        rw = lax.broadcasted_iota(jnp.int32, (tq, WIDTH), 1)
        out = jnp.zeros((tq, WIDTH), F32)
        for h in range(HEADS):
            out = jnp.where(rw // HEAD_DIM == h, acc_ref[h * tq:(h + 1) * tq], out)
        o_ref[0] = out


def _sb_paged(q, kt_new, vt_new, cache_kt, cache_vt, page_table, bias, layer):
    bsz, tq, _ = q.shape
    n_pages = page_table.shape[1]
    group = next(g for g in (PAGE_GROUP, 4, 2, 1) if n_pages % g == 0)
    n_steps = n_pages // group
    rows = HEADS * tq
    per_b = lambda b, j, pt: (b, 0, 0)

    def page(g):
        return lambda b, j, pt: (layer, pt[b, n_pages - 1 - (j * group + g)], 0, 0)

    page_specs = [pl.BlockSpec((1, 1, WIDTH, PAGE_SIZE), page(g)) for g in range(group)]
    grid_spec = pltpu.PrefetchScalarGridSpec(
        num_scalar_prefetch=1,
        grid=(bsz, n_steps),
        in_specs=[pl.BlockSpec(memory_space=pltpu.SMEM),
                  pl.BlockSpec((1, tq, WIDTH), per_b),
                  pl.BlockSpec((1, WIDTH, PAGE_SIZE), per_b),
                  pl.BlockSpec((1, WIDTH, PAGE_SIZE), per_b)] + page_specs + page_specs,
        out_specs=pl.BlockSpec((1, tq, WIDTH), per_b),
        scratch_shapes=[pltpu.VMEM((rows, WIDTH), BF16), pltpu.VMEM((rows, PAGE_SIZE), F32),
                        pltpu.VMEM((PAGE_SIZE, PAGE_SIZE), BF16),
                        pltpu.VMEM((rows, WIDTH), F32), pltpu.VMEM((rows, 1), F32)])
    return pl.pallas_call(
        functools.partial(_sb_paged_body, tq=tq, group=group, n_steps=n_steps),
        grid_spec=grid_spec,
        out_shape=jax.ShapeDtypeStruct((bsz, tq, WIDTH), F32),
        compiler_params=_params(("parallel", "arbitrary")),
        name="sb_paged",
    )(page_table, bias, q, kt_new, vt_new, *([cache_kt] * group), *([cache_vt] * group))


def _mlstm_body(q_ref, k_ref, v_ref, og_ref, gc_ref, bc_ref, hg_ref, c0_ref, n0_ref, m0_ref,
                y_ref, c_out_ref, n_out_ref, m_out_ref, c_s, n_s, m_s, *, chunk, n_seq, n_tok, n_real, n_chunks):
    ci = pl.program_id(1)

    @pl.when(ci == 0)
    def _():
        c_s[...] = c0_ref[...]
        n_s[...] = n0_ref[...]
        m_s[...] = m0_ref[...]

    L = chunk
    short = n_tok < L

    def tokens(ref, b, cols=slice(None)):
        x = ref[b, :, cols]
        if short:
            x = jnp.concatenate([x, jnp.zeros((L - n_tok, x.shape[1]), x.dtype)], axis=0)
        return x

    r_i = lax.broadcasted_iota(jnp.int32, (L, L), 0)
    c_i = lax.broadcasted_iota(jnp.int32, (L, L), 1)
    causal = r_i >= c_i
    tri = jnp.where(causal, 1.0, 0.0).astype(F32)
    lane_g = lax.broadcasted_iota(jnp.int32, (L, GATE_LANES), 1)
    lane = lax.broadcasted_iota(jnp.int32, (L, PAIR), 1)
    lane1 = lax.broadcasted_iota(jnp.int32, (1, PAIR), 1)
    blk_r = lax.broadcasted_iota(jnp.int32, (PAIR, PAIR), 0)
    blk_c = lax.broadcasted_iota(jnp.int32, (PAIR, PAIR), 1)
    same_head = (blk_r < HEAD_DIM) == (blk_c < HEAD_DIM)
    lane_m = lax.broadcasted_iota(jnp.int32, (1, HEADS), 1)
    mine = [lane < HEAD_DIM, lane >= HEAD_DIM]
    cols = [slice(p * PAIR, (p + 1) * PAIR) for p in range(N_PAIRS)]
    seqs = range(n_seq)
    probs = [(b, h) for b in seqs for h in range(HEADS)]
    pair_ids = [(b, p) for b in seqs for p in range(N_PAIRS)]

    x_c, cum_c, x_r, cum_r = {}, {}, {}, {}
    for b in seqs:
        gate_c = tokens(gc_ref, b) + bc_ref[...]
        xc = jnp.where(lane_g >= HEADS, _log_sigmoid(gate_c), gate_c)
        if n_real < n_chunks * L:
            tok_c = ci * L + lax.broadcasted_iota(jnp.int32, (L, GATE_LANES), 0)
            xc = jnp.where(tok_c < n_real, xc, jnp.where(lane_g >= HEADS, 0.0, NEG_BIG))
        cc = jnp.dot(tri, xc, preferred_element_type=F32, precision=lax.Precision.HIGHEST)
        x_c[b], cum_c[b] = xc, cc
        x_r[b] = xc.T[:2 * HEADS]
        cum_r[b] = cc.T[:2 * HEADS]

    q2 = {bp: tokens(q_ref, bp[0], cols[bp[1]]) for bp in pair_ids}
    k2 = {bp: tokens(k_ref, bp[0], cols[bp[1]]) for bp in pair_ids}
    k2b = {bp: k2[bp].astype(BF16) for bp in pair_ids}
    v2b = {bp: tokens(v_ref, bp[0], cols[bp[1]]).astype(BF16) for bp in pair_ids}
    c_pair = {bp: c_s[bp[0], bp[1]] for bp in pair_ids}
    n_row = {bp: n_s[bp[0], bp[1]] for bp in pair_ids}
    pair_of = lambda bh: (bh[0], bh[1] // 2)
    qm = {bh: jnp.where(mine[bh[1] % 2], q2[pair_of(bh)], 0.0) for bh in probs}
    qmb = {bh: qm[bh].astype(BF16) for bh in probs}

    b_c, i_c, b_r, i_r, m_prev, m_new, wend, decay = {}, {}, {}, {}, {}, {}, {}, {}
    for bh in probs:
        b, h = bh
        b_c[bh] = cum_c[b][:, HEADS + h:HEADS + h + 1]
        i_c[bh] = x_c[b][:, h:h + 1]
        b_r[bh] = cum_r[b][HEADS + h:HEADS + h + 1, :]
        i_r[bh] = x_r[b][h:h + 1, :]
        m_prev[bh] = m_s[b][:, h:h + 1]
        b_last = b_r[bh][:, L - 1:L]
        m_new[bh] = jnp.maximum(b_last + m_prev[bh],
                                jnp.max(b_last - b_r[bh] + i_r[bh], axis=1, keepdims=True))
        wend[bh] = jnp.exp(b_last - b_c[bh] + i_c[bh] - m_new[bh])
        decay[bh] = jnp.exp(b_last + m_prev[bh] - m_new[bh])
    kw = {(b, p): k2[(b, p)] * jnp.where(mine[1], wend[(b, 2 * p + 1)], wend[(b, 2 * p)]) for b, p in pair_ids}

    qk = {bh: lax.dot_general(qmb[bh], k2b[pair_of(bh)], _NT, preferred_element_type=F32) for bh in probs}
    qc = {bh: jnp.dot(qmb[bh], c_pair[pair_of(bh)].astype(BF16), preferred_element_type=F32) for bh in probs}
    upd = {bp: jnp.dot(kw[bp].T.astype(BF16), v2b[bp], preferred_element_type=F32) for bp in pair_ids}

    s, w_inter, m_t = {}, {}, {}
    for bh in probs:
        log_d = jnp.where(causal, b_c[bh] - b_r[bh] + i_r[bh], NEG_BIG)
        inter = b_c[bh] + m_prev[bh]
        m_t[bh] = jnp.maximum(inter, jnp.max(log_d, axis=1, keepdims=True))
        s[bh] = qk[bh] * jnp.exp(log_d - m_t[bh])
        w_inter[bh] = jnp.exp(inter - m_t[bh])
    sv = {bh: jnp.dot(s[bh].astype(BF16), v2b[pair_of(bh)], preferred_element_type=F32) for bh in probs}

    hid = {}
    for bh in probs:
        num = sv[bh] + w_inter[bh] * qc[bh]
        den = (jnp.sum(s[bh], axis=1, keepdims=True)
               + w_inter[bh] * jnp.sum(qm[bh] * n_row[pair_of(bh)], axis=1, keepdims=True))
        x = num * (1.0 / jnp.maximum(jnp.abs(den), jnp.exp(-m_t[bh])))
        ms = jnp.sum(jnp.where(mine[bh[1] % 2], x * x, 0.0), axis=1, keepdims=True) * (1.0 / HEAD_DIM)
        hid[bh] = x * lax.rsqrt(ms + RMS_EPS)

    for b in seqs:
        m_next = m_s[b]
        for h in range(HEADS):
            m_next = jnp.where(lane_m == h, m_new[(b, h)], m_next)
        m_s[b] = m_next
    for b, p in pair_ids:
        h_pair = jnp.where(mine[1], hid[(b, 2 * p + 1)], hid[(b, 2 * p)])
        y = tokens(og_ref, b, cols[p]) * h_pair * hg_ref[:, cols[p]]
        y_ref[b, :, cols[p]] = y[:n_tok].astype(y_ref.dtype) if short else y.astype(y_ref.dtype)
        decay_pair = jnp.where(lane1 >= HEAD_DIM, decay[(b, 2 * p + 1)], decay[(b, 2 * p)])
        c_s[b, p] = decay_pair * c_pair[(b, p)] + jnp.where(same_head, upd[(b, p)], 0.0)
        n_s[b, p] = decay_pair * n_row[(b, p)] + jnp.sum(kw[(b, p)], axis=0, keepdims=True)

    @pl.when(ci == n_chunks - 1)
    def _():
        c_out_ref[...] = c_s[...]
        n_out_ref[...] = n_s[...]
        m_out_ref[...] = m_s[...]


def _pack_pairs(c):
    bsz = c.shape[0]
    c = c.reshape(bsz, N_PAIRS, 2, HEAD_DIM, HEAD_DIM)
    z = jnp.zeros_like(c[:, :, 0])
    top = jnp.concatenate([c[:, :, 0], z], axis=-1)
    bot = jnp.concatenate([z, c[:, :, 1]], axis=-1)
    return jnp.concatenate([top, bot], axis=-2)


def _unpack_pairs(cp):
    bsz = cp.shape[0]
    a = cp[:, :, :HEAD_DIM, :HEAD_DIM]
    b = cp[:, :, HEAD_DIM:, HEAD_DIM:]
    return jnp.stack([a, b], axis=2).reshape(bsz, HEADS, HEAD_DIM, HEAD_DIM)


def _mlstm(q, k, v, og, gates, bias_c, head_gain, c0, n0, m0, n_real):
    bsz, t, _ = q.shape
    L = ML_CHUNK
    n_chunks = max(t // L, 1)
    tb = min(t, L)
    assert t == n_chunks * tb
    nb = next(n for n in (2 * ML_SEQS_PER_STEP, ML_SEQS_PER_STEP, 1) if bsz % n == 0)
    y_dtype = BF16 if tb == L else F32
    tok = lambda b, c: (b, c, 0)
    per_b3 = lambda b, c: (b, 0, 0)
    per_b4 = lambda b, c: (b, 0, 0, 0)
    y, c_out, n_out, m_out = pl.pallas_call(
        functools.partial(_mlstm_body, chunk=L, n_seq=nb, n_tok=tb, n_real=n_real, n_chunks=n_chunks),
        grid=(bsz // nb, n_chunks),
        in_specs=[pl.BlockSpec((nb, tb, WIDTH), tok)] * 4 + [
            pl.BlockSpec((nb, tb, GATE_LANES), tok),
            pl.BlockSpec((1, GATE_LANES), lambda b, c: (0, 0)),
            pl.BlockSpec((1, WIDTH), lambda b, c: (0, 0)),
            pl.BlockSpec((nb, N_PAIRS, PAIR, PAIR), per_b4),
            pl.BlockSpec((nb, N_PAIRS, 1, PAIR), per_b4),
            pl.BlockSpec((nb, 1, HEADS), per_b3)],
        out_specs=[pl.BlockSpec((nb, tb, WIDTH), tok),
                   pl.BlockSpec((nb, N_PAIRS, PAIR, PAIR), per_b4),
                   pl.BlockSpec((nb, N_PAIRS, 1, PAIR), per_b4),
                   pl.BlockSpec((nb, 1, HEADS), per_b3)],
        out_shape=[jax.ShapeDtypeStruct((bsz, t, WIDTH), y_dtype),
                   jax.ShapeDtypeStruct((bsz, N_PAIRS, PAIR, PAIR), F32),
                   jax.ShapeDtypeStruct((bsz, N_PAIRS, 1, PAIR), F32),
                   jax.ShapeDtypeStruct((bsz, 1, HEADS), F32)],
        scratch_shapes=[pltpu.VMEM((nb, N_PAIRS, PAIR, PAIR), F32), pltpu.VMEM((nb, N_PAIRS, 1, PAIR), F32),
                        pltpu.VMEM((nb, 1, HEADS), F32)],
        compiler_params=_params(("parallel", "arbitrary")),
        name="mlstm",
    )(q, k, v, og, gates, bias_c, head_gain,
      _pack_pairs(c0), n0.reshape(bsz, N_PAIRS, 1, PAIR), m0.reshape(bsz, 1, HEADS))
    return (y, _unpack_pairs(c_out), n_out.reshape(bsz, HEADS, HEAD_DIM), m_out.reshape(bsz, HEADS))


def _merge_body(*refs, conv_here):
    if conv_here:
        x_ref, ysb_ref, yml_ref, cb_ref, u_ref, u1_ref, u2_ref, mg_ref, cw_ref, wbr_ref, wout_ref, o_ref = refs
        cw = cw_ref[...]
        conv = cw[0:1] * u2_ref[...] + cw[1:2] * u1_ref[...] + cw[2:3] * u_ref[...]
        ycv = (cb_ref[...] * conv).astype(BF16)
    else:
        x_ref, ysb_ref, yml_ref, ycv_ref, mg_ref, wbr_ref, wout_ref, o_ref = refs
        ycv = ycv_ref[...]
    gate = lambda j: mg_ref[:, j * D_MODEL:(j + 1) * D_MODEL].astype(F32)
    merged = (gate(0) * jnp.dot(ysb_ref[...], wbr_ref[0], preferred_element_type=F32)
              + gate(1) * jnp.dot(yml_ref[...], wbr_ref[1], preferred_element_type=F32)
              + gate(2) * jnp.dot(ycv, wbr_ref[2], preferred_element_type=F32))
    o_ref[...] = x_ref[...] + jnp.dot(merged.astype(BF16), wout_ref[...], preferred_element_type=F32)


def _merge(x, ysb, yml, conv_inputs, mg, w_br, w_out):
    m = x.shape[0]
    tm = _row_tile(m, ROW_TILE_PROJ)
    row = lambda i: (i, 0)
    wide = pl.BlockSpec((tm, WIDTH), row)
    conv_here = len(conv_inputs) > 1
    conv_specs = [wide] * 4 if conv_here else [wide]
    weights = ([conv_inputs[4]] if conv_here else []) + [w_br, w_out]
    return pl.pallas_call(
        functools.partial(_merge_body, conv_here=conv_here),
        grid=(m // tm,),
        in_specs=[pl.BlockSpec((tm, D_MODEL), row), wide, wide] + conv_specs + [
            pl.BlockSpec((tm, 3 * D_MODEL), row)] + [_resident(w.shape) for w in weights],
        out_specs=pl.BlockSpec((tm, D_MODEL), row),
        out_shape=jax.ShapeDtypeStruct((m, D_MODEL), F32),
        compiler_params=_params(("parallel",)),
        name="merge",
    )(x, ysb, yml, *conv_inputs[:4], mg, *weights)


def _shifted(u, buf):
    ext = jnp.concatenate([buf, u], axis=1)
    t = u.shape[1]
    return ext[:, 1:t + 1], ext[:, 0:t]


def kernel(x_prompt, x_sample, cache_sb_k, cache_sb_v, page_table, state_ml_C, state_ml_n, state_ml_m,
           state_conv, meta_tokens, norm_ff1, ffn1_w_up, ffn1_w_down, norm_mix, w_in, sb_logit_bias,
           ml_igate_bias, ml_fgate_bias, ml_head_norm, conv_w, w_branch_sb, w_branch_ml, w_branch_cv, w_out,
           norm_ff2, ffn2_w_up, ffn2_w_down, norm_final):
    bp, seq, _ = x_prompt.shape
    bs, ts, _ = x_sample.shape
    depth = w_in.shape[0]
    d_ff = ffn1_w_down.shape[1]
    n_ff = d_ff // FF_CHUNK
    n_small = bs * ts
    blk = ATT_BLOCK
    nt = N_META + seq
    tf = -(-nt // blk) * blk
    assert blk % ML_CHUNK == 0 and d_ff % FF_CHUNK == 0
    assert ts % 8 == 0 and ts <= ML_CHUNK

    def ffn_weights(w_up, w_down):
        gate = w_up[:, :d_ff].reshape(D_MODEL, n_ff, FF_CHUNK)
        up = w_up[:, d_ff:].reshape(D_MODEL, n_ff, FF_CHUNK)
        wu = jnp.concatenate([gate, up], axis=-1).transpose(1, 0, 2).astype(BF16)
        return wu, w_down.reshape(n_ff, FF_CHUNK, D_MODEL).astype(BF16)

    def in_weights(w):
        main = jnp.concatenate([w[:, :6 * WIDTH], w[:, 6 * WIDTH + 2 * HEADS:]], axis=1)
        w_seg = main.reshape(D_MODEL, N_SEG, WIDTH).transpose(1, 0, 2).astype(BF16)
        gate = w[:, 6 * WIDTH:6 * WIDTH + 2 * HEADS]
        w_gate = jnp.pad(gate, ((0, 0), (0, GATE_LANES - 2 * HEADS))).astype(BF16)
        return w_seg, w_gate

    row1 = lambda a: a.reshape(1, -1)
    meta = jnp.broadcast_to(meta_tokens.astype(F32)[None], (bp, N_META, D_MODEL))
    xb = jnp.concatenate([meta, x_prompt, jnp.zeros((bp, tf - nt, D_MODEL), F32)], axis=1).reshape(bp * tf, D_MODEL)
    xs = x_sample.reshape(n_small, D_MODEL)
    pages_t = lambda c: jnp.transpose(c, (0, 1, 3, 4, 2)).reshape(c.shape[:2] + (WIDTH, PAGE_SIZE))
    cache_kt = pages_t(cache_sb_k)
    cache_vt = pages_t(cache_sb_v)
    frame = lambda a: a.reshape(bp, tf, a.shape[-1])
    smp = lambda a: a.reshape(bs, ts, a.shape[-1])
    flat = lambda a: a.reshape(-1, a.shape[-1])
    heads = lambda a: a.reshape(a.shape[:-1] + (HEADS, HEAD_DIM))

    new_p, new_s = [], []
    for l in range(depth):
        last = l == depth - 1
        wu1, wd1 = ffn_weights(ffn1_w_up[l], ffn1_w_down[l])
        wu2, wd2 = ffn_weights(ffn2_w_up[l], ffn2_w_down[l])
        w_seg, w_gate = in_weights(w_in[l])
        w_br = jnp.stack([w_branch_sb[l], w_branch_ml[l], w_branch_cv[l]]).astype(BF16)
        w_o = w_out[l].astype(BF16)
        bias_c = jnp.pad(jnp.concatenate([ml_igate_bias[l], ml_fgate_bias[l]]),
                         (0, GATE_LANES - 2 * HEADS)).reshape(1, GATE_LANES)
        ml_args = (bias_c, row1(ml_head_norm[l]))

        xb = _ffn(xb, row1(norm_ff1[l]), wu1, wd1)
        xs = _ffn(xs, row1(norm_ff1[l]), wu1, wd1)
        (qa_b, kn_b, ka_b, vn_b, va_b, mq_b, mk_b, mv_b, og_b, ycv_b, ulast_b, mg_b, ig_b) = _in_proj(
            xb, row1(norm_mix[l]), w_seg, w_gate, conv_w[l], (bp, tf, nt))
        (qa_s, kn_s, ka_s, vn_s, va_s, mq_s, mk_s, mv_s, og_s, cb_s, u_s, mg_s, ig_s) = _in_proj(
            xs, row1(norm_mix[l]), w_seg, w_gate)

        ysb_b = flat(_sb_prompt(frame(qa_b), frame(ka_b), frame(va_b), sb_logit_bias[l], blk))
        new_t = lambda a: jnp.pad(jnp.swapaxes(smp(a), 1, 2), ((0, 0), (0, 0), (0, PAGE_SIZE - ts)))
        ysb_s = flat(_sb_paged(smp(qa_s).astype(F32), new_t(kn_s), new_t(vn_s), cache_kt, cache_vt, page_table,
                               sb_logit_bias[l], l)).astype(BF16)

        yml_b, c_p, n_p, m_p = _mlstm(
            frame(mq_b), frame(mk_b), frame(mv_b), frame(og_b), frame(ig_b), *ml_args,
            jnp.zeros((bp, HEADS, HEAD_DIM, HEAD_DIM), F32), jnp.zeros((bp, HEADS, HEAD_DIM), F32),
            jnp.zeros((bp, HEADS), F32), nt)
        yml_s, c_s, n_s, m_s = _mlstm(
            smp(mq_s), smp(mk_s), smp(mv_s), smp(og_s), smp(ig_s), *ml_args,
            state_ml_C[l].astype(F32), state_ml_n[l].astype(F32), state_ml_m[l].astype(F32), ts)

        u_smp = smp(u_s)
        u1_s, u2_s = _shifted(u_smp, state_conv[l].astype(F32))
        xb = _merge(xb, ysb_b, flat(yml_b), (ycv_b,), mg_b, w_br, w_o)
        xs = _merge(xs, ysb_s, flat(yml_s).astype(BF16), (cb_s, u_s, flat(u1_s), flat(u2_s), conv_w[l]), mg_s,
                    w_br, w_o)

        fin = row1(norm_final) if last else None
        xb = _ffn(xb, row1(norm_ff2[l]), wu2, wd2, fin)
        xs = _ffn(xs, row1(norm_ff2[l]), wu2, wd2, fin)

        new_p.append((heads(kn_b), heads(vn_b), c_p, n_p, m_p, ulast_b[:, 6:]))
        new_s.append((heads(smp(kn_s)), heads(smp(vn_s)), c_s, n_s, m_s,
                      jnp.concatenate([state_conv[l].astype(F32), u_smp], axis=1)[:, -2:]))

    y_prompt = frame(xb)[:, N_META:nt]
    y_sample = xs.reshape(bs, ts, D_MODEL)
    stk = lambda states, i: jnp.stack([st[i] for st in states])
    return (y_prompt, y_sample) + tuple(stk(new_p, i) for i in range(6)) + tuple(stk(new_s, i) for i in range(6))
```
